```python
import jax
import jax.numpy as jnp
from jax import lax
import numpy as np

D_MODEL = 1024
BATCH = 2
SEQ = 16384
DEPTH = 4
DEC_BATCH = 32
DEC_SEQ = 16
PAST_LEN = 2048

CHUNK = 64
Q_BLOCK = 128
N_MIXERS = 2
N_HGRN = (DEPTH + 1) // 2
N_DSA = DEPTH // 2
D_FF = 2816
HG_HEADS = 8
HG_DK = 128
HG_DV = D_MODEL // HG_HEADS
ATT_HEADS = 8
ATT_KV_HEADS = 2
ATT_GROUP = ATT_HEADS // ATT_KV_HEADS
ATT_DH = 128
IDX_HEADS = 8
IDX_DIM = 64
TOPK_MAX = 256
ROPE_THETA = 10000.0
EPS = 1e-6
HG_IN = 2 * HG_HEADS * HG_DK + 2 * HG_HEADS * HG_DV
HG_SPLITS = (HG_HEADS * HG_DK, 2 * HG_HEADS * HG_DK, 2 * HG_HEADS * HG_DK + HG_HEADS * HG_DV)
_QW = ATT_HEADS * ATT_DH
_KVW = ATT_KV_HEADS * ATT_DH
_QIW = IDX_HEADS * IDX_DIM
DSA_SPLITS = (_QW, _QW + _KVW, _QW + 2 * _KVW, _QW + 2 * _KVW + _QIW, _QW + 2 * _KVW + _QIW + IDX_DIM)
DSA_IN = _QW + 2 * _KVW + _QIW + IDX_DIM + IDX_HEADS

kernel_name = 'hybrid_hgrn2_dsa_streaming_step'


def rmsnorm(x, g):
    x32 = x.astype(jnp.float32)
    y = x32 * lax.rsqrt(jnp.mean(x32 * x32, axis=-1, keepdims=True) + EPS)
    return (y * g.astype(jnp.float32)).astype(x.dtype)


def rope(x, pos):
    d = x.shape[-1]
    half = d // 2
    inv_freq = jnp.power(ROPE_THETA, -jnp.arange(half, dtype=jnp.float32) * (2.0 / d))
    ang = pos.astype(jnp.float32)[:, None] * inv_freq[None, :]
    cos = jnp.cos(ang)[:, None, :]
    sin = jnp.sin(ang)[:, None, :]
    x32 = x.astype(jnp.float32)
    x1, x2 = x32[..., :half], x32[..., half:]
    return jnp.concatenate([x1 * cos - x2 * sin, x2 * cos + x1 * sin], axis=-1).astype(x.dtype)


def swiglu_ffn(x, g, w_in, w_out):
    h = rmsnorm(x, g) @ w_in
    a, b = jnp.split(h, 2, axis=-1)
    return (jax.nn.silu(a) * b) @ w_out


def hgrn_lower_bounds(logits):
    p = jax.nn.softmax(logits.astype(jnp.float32), axis=0)
    return jnp.maximum(jnp.cumsum(p, axis=0) - p[:1], 0.0)


def hgrn_chunk_step(S, inp):
    q, k, v, logf = inp
    C = q.shape[2]
    b = jnp.cumsum(logf, axis=2)
    causal = jnp.tril(jnp.ones((C, C), dtype=bool))
    diff = b[:, :, :, None, :] - b[:, :, None, :, :]
    decay = jnp.exp(jnp.where(causal[None, None, :, :, None], diff, -jnp.inf))
    scores = jnp.einsum('bhtd,bhsd,bhtsd->bhts', q, k, decay)
    o = jnp.einsum('bhts,bhse->bhte', scores, v) + jnp.einsum('bhtd,bhde->bhte', q * jnp.exp(b), S)
    b_last = b[:, :, -1:, :]
    S_new = jnp.exp(b_last[:, :, 0, :])[..., None] * S + jnp.einsum('bhsd,bhse->bhde', k * jnp.exp(b_last - b), v)
    return S_new, o


def hgrn_mixer(xn, S0, w_in, lb, onorm, w_out, chunk):
    B, L, _ = xn.shape
    n = L // chunk
    h = xn @ w_in
    q, f, i, g = jnp.split(h, HG_SPLITS, axis=-1)
    q = jax.nn.silu(q.astype(jnp.float32))
    logf = jnp.logaddexp(jnp.log(lb), jnp.log1p(-lb) + jax.nn.log_sigmoid(f.astype(jnp.float32)))
    k = -jnp.expm1(logf)

    def heads(t, d):
        return t.astype(jnp.float32).reshape(B, n, chunk, HG_HEADS, d).transpose(1, 0, 3, 2, 4)

    S_fin, o = lax.scan(hgrn_chunk_step, S0.astype(jnp.float32),
                        (heads(q, HG_DK), heads(k, HG_DK), heads(i, HG_DV), heads(logf, HG_DK)))
    o = o.transpose(1, 0, 3, 2, 4).reshape(B, L, HG_HEADS, HG_DV)
    o = rmsnorm(o, onorm).reshape(B, L, HG_HEADS * HG_DV) * jax.nn.silu(g.astype(jnp.float32))
    return o.astype(xn.dtype) @ w_out, S_fin.astype(xn.dtype)


def dsa_project(xn, w_in, qn, kn, pos):
    B, L, _ = xn.shape
    h = xn @ w_in
    q, k, v, qi, ki, wi = jnp.split(h, DSA_SPLITS, axis=-1)
    q = rope(rmsnorm(q.reshape(B, L, ATT_HEADS, ATT_DH), qn), pos)
    k = rope(rmsnorm(k.reshape(B, L, ATT_KV_HEADS, ATT_DH), kn), pos)
    v = v.reshape(B, L, ATT_KV_HEADS, ATT_DH)
    qi = rope(qi.reshape(B, L, IDX_HEADS, IDX_DIM), pos)
    ki = rope(ki.reshape(B, L, 1, IDX_DIM), pos)[:, :, 0]
    wi = wi * ((IDX_HEADS ** -0.5) * (IDX_DIM ** -0.5))
    return q, k, v, qi, ki, wi


def sparse_attend(q, qi, wi, limit, k, v, ki, topk):
    B, Q = q.shape[0], q.shape[1]
    Lk = k.shape[1]
    s = jax.nn.relu(jnp.einsum('bqhd,bld->bqhl', qi.astype(jnp.float32), ki.astype(jnp.float32)))
    score = jnp.einsum('bqhl,bqh->bql', s, wi.astype(jnp.float32))
    adm = jnp.arange(Lk)[None, :] < limit[:, None]
    score = jnp.where(adm[None], score, -jnp.inf)
    _, sel = lax.top_k(score, topk)
    valid = sel < limit[None, :, None]
    gather = jax.vmap(lambda rows, idx: rows[idx])
    k_sel = gather(k, sel).astype(jnp.float32)
    v_sel = gather(v, sel).astype(jnp.float32)
    qg = q.reshape(B, Q, ATT_KV_HEADS, ATT_GROUP, ATT_DH).astype(jnp.float32)
    logits = jnp.einsum('bqngd,bqknd->bqngk', qg, k_sel) * (ATT_DH ** -0.5)
    logits = jnp.where(valid[:, :, None, None, :], logits, -jnp.inf)
    p = jax.nn.softmax(logits, axis=-1)
    o = jnp.einsum('bqngk,bqknd->bqngd', p, v_sel)
    return o.reshape(B, Q, ATT_HEADS * ATT_DH).astype(q.dtype)


def dsa_prompt(xn, w_in, qn, kn, w_out, pos):
    q, k, v, qi, ki, wi = dsa_project(xn, w_in, qn, kn, pos)
    B, L = xn.shape[0], xn.shape[1]
    nb = L // Q_BLOCK
    topk = min(TOPK_MAX, L // 4)
    limit = ((pos // CHUNK) + 1) * CHUNK

    def blk(t):
        return t.reshape((B, nb, Q_BLOCK) + t.shape[2:]).swapaxes(0, 1)

    o = lax.map(lambda a: sparse_attend(a[0], a[1], a[2], a[3], k, v, ki, topk),
                (blk(q), blk(qi), blk(wi), limit.reshape(nb, Q_BLOCK)))
    o = o.swapaxes(0, 1).reshape(B, L, ATT_HEADS * ATT_DH)
    return o @ w_out, k, v, ki


def dsa_sample(xn, ck, cv, cki, w_in, qn, kn, w_out, pos):
    q, k, v, qi, ki, wi = dsa_project(xn, w_in, qn, kn, pos)
    k_all = jnp.concatenate([ck.astype(k.dtype), k], axis=1)
    v_all = jnp.concatenate([cv.astype(v.dtype), v], axis=1)
    ki_all = jnp.concatenate([cki.astype(ki.dtype), ki], axis=1)
    Lk = k_all.shape[1]
    topk = min(TOPK_MAX, Lk // 4)
    limit = jnp.full((xn.shape[1],), Lk, dtype=jnp.int32)
    o = sparse_attend(q, qi, wi, limit, k_all, v_all, ki_all, topk)
    return o @ w_out, k, v, ki


def setup_inputs(seed: int = 0) -> dict:
    key = jax.random.key(seed)
    ks = jax.random.split(key, 21)
    f32 = jnp.float32

    def nrm(k, shape, scale):
        return jax.random.normal(k, shape, f32) * scale

    def gain(k, shape):
        return 1.0 + 0.02 * jax.random.normal(k, shape, f32)

    return {
        'x_prompt': nrm(ks[0], (BATCH, SEQ, D_MODEL), 1.0),
        'x_sample': nrm(ks[1], (DEC_BATCH, DEC_SEQ, D_MODEL), 1.0),
        'state_hgrn': nrm(ks[2], (N_HGRN, DEC_BATCH, HG_HEADS, HG_DK, HG_DV), 0.5),
        'cache_k': nrm(ks[3], (N_DSA, DEC_BATCH, PAST_LEN, ATT_KV_HEADS, ATT_DH), 1.0),
        'cache_v': nrm(ks[4], (N_DSA, DEC_BATCH, PAST_LEN, ATT_KV_HEADS, ATT_DH), 1.0),
        'cache_kidx': nrm(ks[5], (N_DSA, DEC_BATCH, PAST_LEN, IDX_DIM), 1.0),
        'norm_ffn1': gain(ks[6], (DEPTH, D_MODEL)),
        'ffn1_w_in': nrm(ks[7], (DEPTH, D_MODEL, 2 * D_FF), D_MODEL ** -0.5),
        'ffn1_w_out': nrm(ks[8], (DEPTH, D_FF, D_MODEL), D_FF ** -0.5),
        'norm_mix': gain(ks[9], (DEPTH, D_MODEL)),
        'norm_ffn2': gain(ks[10], (DEPTH, D_MODEL)),
        'ffn2_w_in': nrm(ks[11], (DEPTH, D_MODEL, 2 * D_FF), D_MODEL ** -0.5),
        'ffn2_w_out': nrm(ks[12], (DEPTH, D_FF, D_MODEL), D_FF ** -0.5),
        'hgrn_w_in': nrm(ks[13], (N_HGRN, D_MODEL, HG_IN), D_MODEL ** -0.5),
        'hgrn_lb_logits': nrm(ks[14], (N_HGRN, HG_HEADS * HG_DK), 0.5),
        'hgrn_onorm': gain(ks[15], (N_HGRN, HG_DV)),
        'hgrn_w_out': nrm(ks[16], (N_HGRN, HG_HEADS * HG_DV, D_MODEL), (HG_HEADS * HG_DV) ** -0.5),
        'dsa_w_in': nrm(ks[17], (N_DSA, D_MODEL, DSA_IN), D_MODEL ** -0.5),
        'dsa_qnorm': gain(ks[18], (N_DSA, ATT_DH)),
        'dsa_knorm': gain(ks[19], (N_DSA, ATT_DH)),
        'dsa_w_out': nrm(ks[20], (N_DSA, ATT_HEADS * ATT_DH, D_MODEL), (ATT_HEADS * ATT_DH) ** -0.5),
    }


def reference(x_prompt, x_sample, state_hgrn, cache_k, cache_v, cache_kidx,
              norm_ffn1, ffn1_w_in, ffn1_w_out, norm_mix, norm_ffn2, ffn2_w_in, ffn2_w_out,
              hgrn_w_in, hgrn_lb_logits, hgrn_onorm, hgrn_w_out,
              dsa_w_in, dsa_qnorm, dsa_knorm, dsa_w_out):
    lb_all = hgrn_lower_bounds(hgrn_lb_logits)
    bp, lp = x_prompt.shape[0], x_prompt.shape[1]
    past, ls = cache_k.shape[2], x_sample.shape[1]
    pos_p = jnp.arange(lp, dtype=jnp.int32)
    pos_s = past + jnp.arange(ls, dtype=jnp.int32)
    xp, xs = x_prompt, x_sample
    hg_p, hg_s = [], []
    kp, vp, kip, ksm, vsm, kism = [], [], [], [], [], []
    for i in range(DEPTH):
        xp = xp + 0.5 * swiglu_ffn(xp, norm_ffn1[i], ffn1_w_in[i], ffn1_w_out[i])
        xs = xs + 0.5 * swiglu_ffn(xs, norm_ffn1[i], ffn1_w_in[i], ffn1_w_out[i])
        j = i // N_MIXERS
        xnp = rmsnorm(xp, norm_mix[i])
        xns = rmsnorm(xs, norm_mix[i])
        if i % N_MIXERS == 0:
            s0 = jnp.zeros((bp, HG_HEADS, HG_DK, HG_DV), jnp.float32)
            mp, sp = hgrn_mixer(xnp, s0, hgrn_w_in[j], lb_all[j], hgrn_onorm[j], hgrn_w_out[j], CHUNK)
            ms, ss = hgrn_mixer(xns, state_hgrn[j], hgrn_w_in[j], lb_all[j], hgrn_onorm[j], hgrn_w_out[j], ls)
            hg_p.append(sp)
            hg_s.append(ss)
        else:
            mp, k1, v1, ki1 = dsa_prompt(xnp, dsa_w_in[j], dsa_qnorm[j], dsa_knorm[j], dsa_w_out[j], pos_p)
            ms, k2, v2, ki2 = dsa_sample(xns, cache_k[j], cache_v[j], cache_kidx[j], dsa_w_in[j],
                                         dsa_qnorm[j], dsa_knorm[j], dsa_w_out[j], pos_s)
            kp.append(k1)
            vp.append(v1)
            kip.append(ki1)
            ksm.append(k2)
            vsm.append(v2)
            kism.append(ki2)
        xp = xp + mp
        xs = xs + ms
        xp = xp + 0.5 * swiglu_ffn(xp, norm_ffn2[i], ffn2_w_in[i], ffn2_w_out[i])
        xs = xs + 0.5 * swiglu_ffn(xs, norm_ffn2[i], ffn2_w_in[i], ffn2_w_out[i])
    new_state_hgrn_prompt = jnp.stack(hg_p)
    new_k_prompt = jnp.stack(kp)
    new_v_prompt = jnp.stack(vp)
    new_kidx_prompt = jnp.stack(kip)
    new_state_hgrn_sample = jnp.stack(hg_s)
    new_k_sample = jnp.stack(ksm)
    new_v_sample = jnp.stack(vsm)
    new_kidx_sample = jnp.stack(kism)
    return (xp, xs, new_state_hgrn_prompt, new_k_prompt, new_v_prompt, new_kidx_prompt,
            new_state_hgrn_sample, new_k_sample, new_v_sample, new_kidx_sample)
```

```python
import functools

import numpy as np
import jax
import jax.numpy as jnp
from jax import lax
from jax.experimental import pallas as pl
from jax.experimental.pallas import tpu as pltpu

EPS = 1e-6
CHUNK = 64
HG_HEADS = 8
HG_DK = 128
ATT_HEADS = 8
ATT_KV_HEADS = 2
ATT_GROUP = ATT_HEADS // ATT_KV_HEADS
ATT_DH = 128
IDX_HEADS = 8
IDX_DIM = 64
TOPK_MAX = 256
ROPE_THETA = 10000.0

LANES = 128
SUB = 16
VMEM_LIMIT = 56 * 1024 * 1024

INT_MIN = -2 ** 31
KEY_NEG_INF = -2 ** 31 + 0x7FFFFF
NEG_BIG = -1e30
LOGF_FLOOR = 1e-37

_F32 = jnp.float32
_BF16 = jnp.bfloat16


def _params(*sem):
    return pltpu.CompilerParams(dimension_semantics=sem, vmem_limit_bytes=VMEM_LIMIT)


def _resident(shape):
    nd = len(shape)
    return pl.BlockSpec(shape, lambda *_: (0,) * nd, pipeline_mode=pl.Buffered(1))


def _row_tile(t):
    for tm in (512, 256, 128, 64, 32, 16, 8):
        if t % tm == 0:
            return tm
    raise ValueError(f"unsupported row count {t}")


def _ffn_kernel(x_ref, g_ref, wa_ref, wb_ref, wo_ref, o_ref, acc_ref, *, n_chunks):
    x = x_ref[...]
    ms = jnp.mean(x * x, axis=-1, keepdims=True)
    xn = (x * lax.rsqrt(ms + EPS) * g_ref[...]).astype(_BF16)
    acc_ref[...] = jnp.zeros_like(acc_ref)

    def body(j, carry):
        a = jnp.dot(xn, wa_ref[j], preferred_element_type=_F32)
        b = jnp.dot(xn, wb_ref[j], preferred_element_type=_F32)
        h = (a * jax.nn.sigmoid(a) * b).astype(_BF16)
        acc_ref[...] += jnp.dot(h, wo_ref[j], preferred_element_type=_F32)
        return carry

    lax.fori_loop(0, n_chunks, body, 0)
    o_ref[...] = x + 0.5 * acc_ref[...]


def _ffn_weights(w_in, w_out, tf=256):
    d, two_ff = w_in.shape
    d_ff = two_ff // 2
    n = d_ff // tf
    w = w_in.astype(_BF16).reshape(d, 2, n, tf).transpose(1, 2, 0, 3)
    return w[0], w[1], w_out.astype(_BF16).reshape(n, tf, d)


def _ffn(x, g, wa, wb, wo):
    t, d = x.shape
    n, _, tf = wa.shape
    tm = _row_tile(t)
    return pl.pallas_call(
        functools.partial(_ffn_kernel, n_chunks=n),
        grid=(t // tm,),
        in_specs=[pl.BlockSpec((tm, d), lambda i: (i, 0)),
                  _resident((1, d)), _resident((n, d, tf)), _resident((n, d, tf)),
                  _resident((n, tf, d))],
        out_specs=pl.BlockSpec((tm, d), lambda i: (i, 0)),
        out_shape=jax.ShapeDtypeStruct((t, d), _F32),
        scratch_shapes=[pltpu.VMEM((tm, d), _F32)],
        compiler_params=_params("arbitrary"),
    )(x, g.reshape(1, d), wa, wb, wo)


def _linear_kernel(*refs, has_norm, has_res):
    refs = list(refs)
    x_ref = refs.pop(0)
    g_ref = refs.pop(0) if has_norm else None
    w_ref = refs.pop(0)
    r_ref = refs.pop(0) if has_res else None
    o_ref = refs.pop(0)
    x = x_ref[...]
    if has_norm:
        ms = jnp.mean(x * x, axis=-1, keepdims=True)
        x = x * lax.rsqrt(ms + EPS) * g_ref[...]
    y = jnp.dot(x.astype(_BF16), w_ref[...], preferred_element_type=_F32)
    if has_res:
        y = y + r_ref[...]
    o_ref[...] = y


def _linear(x, w, g=None, res=None):
    t, k = x.shape
    n = w.shape[1]
    tm = _row_tile(t)
    tn = n if n <= 2304 else 1024
    in_specs = [pl.BlockSpec((tm, k), lambda i, j: (i, 0))]
    args = [x]
    if g is not None:
        in_specs.append(_resident((1, k)))
        args.append(g.reshape(1, k))
    if tn == n:
        in_specs.append(_resident((k, n)))
    else:
        in_specs.append(pl.BlockSpec((k, tn), lambda i, j: (0, j)))
    args.append(w)
    if res is not None:
        in_specs.append(pl.BlockSpec((tm, tn), lambda i, j: (i, j)))
        args.append(res)
    return pl.pallas_call(
        functools.partial(_linear_kernel, has_norm=g is not None, has_res=res is not None),
        grid=(t // tm, n // tn),
        in_specs=in_specs,
        out_specs=pl.BlockSpec((tm, tn), lambda i, j: (i, j)),
        out_shape=jax.ShapeDtypeStruct((t, n), _F32),
        compiler_params=_params("arbitrary", "arbitrary"),
    )(*args)


def _hgrn_kernel(q_ref, f_ref, v_ref, g_ref, s0_ref, lb_ref, on_ref, ltri_ref, utri_ref,
                 y_ref, sfin_ref,
                 st_sc, qs_sc, ks_sc, qd_sc, kd_sc, b_sc, dec_sc, o_sc, *, n_sub):
    t = pl.program_id(1)

    @pl.when(t == 0)
    def _init():
        for h in range(HG_HEADS):
            st_sc[h] = s0_ref[0, h].T

    lb = lb_ref[...]
    q = q_ref[0]
    fg = lb + (1.0 - lb) * jax.nn.sigmoid(f_ref[0])
    logf = jnp.log(jnp.maximum(fg, LOGF_FLOOR))
    kk = 1.0 - fg
    qs = q * jax.nn.sigmoid(q)
    bcum = jnp.dot(ltri_ref[...], logf, precision=lax.Precision.HIGHEST,
                   preferred_element_type=_F32)
    brem = jnp.dot(utri_ref[...], logf, precision=lax.Precision.HIGHEST,
                   preferred_element_type=_F32)
    qs_sc[...] = qs
    ks_sc[...] = kk
    b_sc[...] = bcum
    qd_sc[...] = qs * jnp.exp(bcum)
    kd_sc[...] = kk * jnp.exp(brem)
    dec_sc[...] = jnp.exp(bcum + brem)

    ones = jnp.ones((HG_DK, LANES), _BF16)
    tio = lax.broadcasted_iota(jnp.int32, (SUB, 1), 0)

    def sub_body(j, carry):
        r0 = pl.multiple_of(j * SUB, SUB)
        rows = pl.ds(r0, SUB)
        for h in range(HG_HEADS):
            cs = slice(h * HG_DK, (h + 1) * HG_DK)
            st = st_sc[h]
            qd = qd_sc[rows, cs]
            acc = lax.dot_general(qd.astype(_BF16), st.astype(_BF16),
                                  (((1,), (1,)), ((), ())), preferred_element_type=_F32)
            bq = b_sc[rows, cs]
            qr = qs_sc[rows, cs]
            kr = ks_sc[rows, cs]
            vv = v_ref[0, rows, cs]
            terms = []
            for s in range(SUB):
                dm = jnp.where(tio >= s, bq - bq[s:s + 1, :], -jnp.inf)
                terms.append(qr * kr[s:s + 1, :] * jnp.exp(dm))
            tcat = jnp.concatenate(terms, axis=0).astype(_BF16)
            rsum = jnp.dot(tcat, ones, preferred_element_type=_F32)
            for s in range(SUB):
                acc = acc + rsum[s * SUB:(s + 1) * SUB, :] * vv[s:s + 1, :]
            o_sc[rows, cs] = acc
            kd = kd_sc[rows, cs]
            upd = jnp.dot(vv.T.astype(_BF16), kd.astype(_BF16), preferred_element_type=_F32)
            st_sc[h] = dec_sc[pl.ds(r0, 1), cs] * st + upd
        return carry

    lax.fori_loop(0, n_sub, sub_body, 0)

    g = g_ref[0]
    gate = g * jax.nn.sigmoid(g)
    for h in range(HG_HEADS):
        cs = slice(h * HG_DK, (h + 1) * HG_DK)
        oh = o_sc[:, cs]
        ms = jnp.mean(oh * oh, axis=-1, keepdims=True)
        y_ref[0, :, cs] = oh * lax.rsqrt(ms + EPS) * on_ref[...] * gate[:, cs]

    @pl.when(t == pl.num_programs(1) - 1)
    def _fin():
        for h in range(HG_HEADS):
            sfin_ref[0, h] = st_sc[h].T


def _hgrn_core(hproj, s0, lb, onorm, tb):
    bsz, l, w4 = hproj.shape
    w = w4 // 4
    dv = w // HG_HEADS
    assert dv == LANES and HG_DK == LANES and tb % SUB == 0 and l % tb == 0
    idx = np.arange(tb)
    same = (idx[:, None] // SUB) == (idx[None, :] // SUB)
    ltri = jnp.asarray((same & (idx[None, :] <= idx[:, None])).astype(np.float32))
    utri = jnp.asarray((same & (idx[None, :] > idx[:, None])).astype(np.float32))

    def col(c):
        return pl.BlockSpec((1, tb, w), lambda b, t: (b, t, c))

    return pl.pallas_call(
        functools.partial(_hgrn_kernel, n_sub=tb // SUB),
        grid=(bsz, l // tb),
        in_specs=[col(0), col(1), col(2), col(3),
                  pl.BlockSpec((1, HG_HEADS, HG_DK, dv), lambda b, t: (b, 0, 0, 0)),
                  _resident((1, w)), _resident((1, dv)), _resident((tb, tb)), _resident((tb, tb))],
        out_specs=[pl.BlockSpec((1, tb, w), lambda b, t: (b, t, 0)),
                   pl.BlockSpec((1, HG_HEADS, HG_DK, dv), lambda b, t: (b, 0, 0, 0))],
        out_shape=[jax.ShapeDtypeStruct((bsz, l, w), _F32),
                   jax.ShapeDtypeStruct((bsz, HG_HEADS, HG_DK, dv), _F32)],
        scratch_shapes=[pltpu.VMEM((HG_HEADS, dv, HG_DK), _F32)] + [pltpu.VMEM((tb, w), _F32)] * 7,
        compiler_params=_params("arbitrary", "arbitrary"),
    )(hproj, hproj, hproj, hproj, s0, lb.reshape(1, w), onorm.reshape(1, dv), ltri, utri)


_QW = ATT_HEADS * ATT_DH
_KVW = ATT_KV_HEADS * ATT_DH
_QIW = IDX_HEADS * IDX_DIM
_KI_OFF = _QW + 2 * _KVW + _QIW
_WI_OFF = _KI_OFF + LANES
_DSA_PAD = _WI_OFF + LANES


def _dsa_post_kernel(h_ref, qn_ref, kn_ref, c1_ref, s1_ref, c2_ref, s2_ref,
                     q_ref, k_ref, v_ref, qi_ref, ki_ref, wi_ref):
    c1, s1, c2, s2 = c1_ref[...], s1_ref[...], c2_ref[...], s2_ref[...]
    first_half = (lax.broadcasted_iota(jnp.int32, (1, LANES), 1) % IDX_DIM) < (IDX_DIM // 2)

    def norm_rope(x, gain):
        ms = jnp.mean(x * x, axis=-1, keepdims=True)
        x = x * lax.rsqrt(ms + EPS) * gain
        return x * c1 + pltpu.roll(x, ATT_DH // 2, 1) * s1

    def rope_idx(x):
        swapped = jnp.where(first_half, pltpu.roll(x, LANES - IDX_DIM // 2, 1),
                            pltpu.roll(x, IDX_DIM // 2, 1))
        return x * c2 + swapped * s2

    for h in range(ATT_HEADS):
        cs = slice(h * ATT_DH, (h + 1) * ATT_DH)
        q_ref[0, :, cs] = (norm_rope(h_ref[0, :, cs], qn_ref[...]) * (ATT_DH ** -0.5)).astype(_BF16)
    for n in range(ATT_KV_HEADS):
        cs = slice(n * ATT_DH, (n + 1) * ATT_DH)
        k_ref[0, :, cs] = norm_rope(h_ref[0, :, _QW + n * ATT_DH:_QW + (n + 1) * ATT_DH], kn_ref[...])
    v_ref[0] = h_ref[0, :, _QW + _KVW:_QW + 2 * _KVW]
    for p in range(_QIW // LANES):
        cs = slice(p * LANES, (p + 1) * LANES)
        qi_ref[0, :, cs] = rope_idx(h_ref[0, :, _QW + 2 * _KVW + p * LANES:_QW + 2 * _KVW + (p + 1) * LANES]).astype(_BF16)
    ki_ref[0] = rope_idx(h_ref[0, :, _KI_OFF:_KI_OFF + LANES])[:, :IDX_DIM]
    wi_ref[0] = h_ref[0, :, _WI_OFF:_WI_OFF + IDX_HEADS] * ((IDX_HEADS ** -0.5) * (IDX_DIM ** -0.5))


def _rope_tables(pos):
    def table(d, reps):
        half = d // 2
        inv_freq = jnp.power(ROPE_THETA, -jnp.arange(half, dtype=_F32) * (2.0 / d))
        ang = pos.astype(_F32)[:, None] * inv_freq[None, :]
        c, s = jnp.cos(ang), jnp.sin(ang)
        return (jnp.tile(jnp.concatenate([c, c], axis=-1), (1, reps)),
                jnp.tile(jnp.concatenate([-s, s], axis=-1), (1, reps)))
    c1, s1 = table(ATT_DH, LANES // ATT_DH)
    c2, s2 = table(IDX_DIM, LANES // IDX_DIM)
    return c1, s1, c2, s2


def _dsa_post(hproj, qn, kn, tables, tb):
    bsz, l, _ = hproj.shape

    def rows(wd):
        return pl.BlockSpec((1, tb, wd), lambda b, t: (b, t, 0))

    tab = pl.BlockSpec((tb, LANES), lambda b, t: (t, 0))
    return pl.pallas_call(
        _dsa_post_kernel,
        grid=(bsz, l // tb),
        in_specs=[rows(_DSA_PAD), _resident((1, ATT_DH)), _resident((1, ATT_DH)), tab, tab, tab, tab],
        out_specs=[rows(_QW), rows(_KVW), rows(_KVW), rows(_QIW), rows(IDX_DIM), rows(IDX_HEADS)],
        out_shape=[jax.ShapeDtypeStruct((bsz, l, _QW), _BF16),
                   jax.ShapeDtypeStruct((bsz, l, _KVW), _F32),
                   jax.ShapeDtypeStruct((bsz, l, _KVW), _F32),
                   jax.ShapeDtypeStruct((bsz, l, _QIW), _BF16),
                   jax.ShapeDtypeStruct((bsz, l, IDX_DIM), _F32),
                   jax.ShapeDtypeStruct((bsz, l, IDX_HEADS), _F32)],
        compiler_params=_params("arbitrary", "arbitrary"),
    )(hproj, qn.reshape(1, ATT_DH), kn.reshape(1, ATT_DH), *tables)


def _dsa_in_weight(w_in):
    d = w_in.shape[0]
    z = lambda n: jnp.zeros((d, n), w_in.dtype)
    ki0 = _QW + 2 * _KVW + _QIW
    return jnp.concatenate([w_in[:, :ki0], w_in[:, ki0:ki0 + IDX_DIM], z(LANES - IDX_DIM),
                            w_in[:, ki0 + IDX_DIM:], z(LANES - IDX_HEADS)], axis=1).astype(_BF16)


def _dsa_attn_kernel(q_ref, qi_ref, wi_ref, k_ref, vt_ref, ki_ref, o_ref, key_sc, *,
                     qb_size, kt, topk, causal, lk_true, n_valid_q):
    qb = pl.program_id(1)
    nq = qb_size
    lane_q = lax.broadcasted_iota(jnp.int32, (1, nq), 1)
    if causal:
        n_tiles = ((qb + 1) * nq) // kt
        limit = ((qb * nq + lane_q) // CHUNK + 1) * CHUNK
    else:
        n_tiles = k_ref.shape[1] // kt
        limit = jnp.full((1, nq), lk_true, jnp.int32)
    row_io = lax.broadcasted_iota(jnp.int32, (kt, 1), 0)

    qi_t = qi_ref[0].astype(_F32).T.astype(_BF16)
    qi_all = jnp.concatenate([qi_t[h * IDX_DIM:(h + 1) * IDX_DIM, :] for h in range(IDX_HEADS)],
                             axis=1)
    wi = wi_ref[0]

    def score_body(t, carry):
        r0 = pl.multiple_of(t * kt, kt)
        d = jnp.dot(ki_ref[0, pl.ds(r0, kt), :], qi_all, preferred_element_type=_F32)
        acc = jnp.zeros((kt, nq), _F32)
        for h in range(IDX_HEADS):
            acc = acc + jnp.maximum(d[:, h * nq:(h + 1) * nq], 0.0) * wi[h:h + 1, :]
        acc = jnp.where(r0 + row_io < limit, acc, -jnp.inf)
        bits = pltpu.bitcast(acc, jnp.int32)
        key_sc[pl.ds(r0, kt), :] = bits ^ ((bits >> 31) & 0x7FFFFFFF)
        return carry

    lax.fori_loop(0, n_tiles, score_body, 0)

    def count_ge(cand):
        def body(t, c):
            r0 = pl.multiple_of(t * kt, kt)
            m = (key_sc[pl.ds(r0, kt), :] >= cand).astype(jnp.int32)
            return c + jnp.sum(m.reshape(kt // 8, 8, nq), axis=0)
        c = lax.fori_loop(0, n_tiles, body, jnp.zeros((8, nq), jnp.int32))
        return jnp.sum(c, axis=0, keepdims=True)

    thr = jnp.where(count_ge(jnp.zeros((1, nq), jnp.int32)) >= topk, 0, INT_MIN).astype(jnp.int32)

    def bit_body(i, thr):
        cand = thr | lax.shift_left(jnp.int32(1), 30 - i)
        return jnp.where(count_ge(cand) >= topk, cand, thr)

    thr = lax.fori_loop(0, 31, bit_body, thr)
    n_ge = count_ge(thr)
    n_gt = count_ge(jnp.where(thr == 2 ** 31 - 1, thr, thr + 1))
    n_gt = jnp.where(thr == 2 ** 31 - 1, 0, n_gt)

    need = (topk - n_gt).astype(_F32)
    over = jnp.max(jnp.where(lane_q < n_valid_q, n_ge, 0)) > topk

    @pl.when(over)
    def _drop_ties():
        rr = lax.broadcasted_iota(jnp.int32, (kt, kt), 0)
        cc = lax.broadcasted_iota(jnp.int32, (kt, kt), 1)
        strict = jnp.where(cc < rr, 1.0, 0.0).astype(_BF16)

        def body(t, run):
            r0 = pl.multiple_of(t * kt, kt)
            kk = key_sc[pl.ds(r0, kt), :]
            tie = kk == thr
            tf = jnp.where(tie, 1.0, 0.0)
            before = jnp.dot(strict, tf.astype(_BF16), preferred_element_type=_F32) + run
            key_sc[pl.ds(r0, kt), :] = jnp.where(tie & (before >= need), KEY_NEG_INF, kk)
            return run + jnp.sum(tf, axis=0, keepdims=True)

        lax.fori_loop(0, n_tiles, body, jnp.zeros((1, nq), _F32))

    thr_sel = jnp.maximum(thr, KEY_NEG_INF + 1)

    q_t = q_ref[0].astype(_F32).T.astype(_BF16)
    gw = ATT_GROUP * nq
    for n in range(ATT_KV_HEADS):
        rhs = jnp.concatenate([q_t[(n * ATT_GROUP + g) * ATT_DH:(n * ATT_GROUP + g + 1) * ATT_DH, :]
                               for g in range(ATT_GROUP)], axis=1)

        def att_body(t, carry, n=n, rhs=rhs):
            m, l, acc = carry
            r0 = pl.multiple_of(t * kt, kt)
            s = jnp.dot(k_ref[0, pl.ds(r0, kt), n * ATT_DH:(n + 1) * ATT_DH], rhs,
                        preferred_element_type=_F32)
            sel = key_sc[pl.ds(r0, kt), :] >= thr_sel
            s = jnp.where(jnp.concatenate([sel] * ATT_GROUP, axis=1), s, NEG_BIG)
            m_new = jnp.maximum(m, jnp.max(s, axis=0, keepdims=True))
            alpha = jnp.exp(m - m_new)
            p = jnp.exp(s - m_new)
            l = alpha * l + jnp.sum(p, axis=0, keepdims=True)
            pv = jnp.dot(vt_ref[0, t, n * ATT_DH:(n + 1) * ATT_DH, :], p.astype(_BF16),
                         preferred_element_type=_F32)
            return m_new, l, alpha * acc + pv

        m, l, acc = lax.fori_loop(
            0, n_tiles, att_body,
            (jnp.full((1, gw), NEG_BIG, _F32), jnp.zeros((1, gw), _F32), jnp.zeros((ATT_DH, gw), _F32)))
        o_t = acc / l
        for g in range(ATT_GROUP):
            hh = n * ATT_GROUP + g
            o_ref[0, :, hh * ATT_DH:(hh + 1) * ATT_DH] = o_t[:, g * nq:(g + 1) * nq].T.astype(_BF16)


def _dsa_attn(q, qi, wi, k, v, ki, *, causal, lk_true, topk, n_valid_q, qb_size=128, kt=128):
    bsz, lq, _ = q.shape
    lk = k.shape[1]
    assert lq % qb_size == 0 and lk % kt == 0 and qb_size % kt == 0
    wi_t = jnp.swapaxes(wi, 1, 2)
    vt = v.astype(_BF16).reshape(bsz, lk // kt, kt, _KVW).transpose(0, 1, 3, 2)

    def whole(shape):
        nd = len(shape)
        return pl.BlockSpec((1,) + shape, lambda b, i: (b,) + (0,) * nd, pipeline_mode=pl.Buffered(1))

    def qrows(wd):
        return pl.BlockSpec((1, qb_size, wd), lambda b, i: (b, i, 0))

    return pl.pallas_call(
        functools.partial(_dsa_attn_kernel, qb_size=qb_size, kt=kt, topk=topk, causal=causal,
                          lk_true=lk_true, n_valid_q=n_valid_q),
        grid=(bsz, lq // qb_size),
        in_specs=[qrows(_QW), qrows(_QIW),
                  pl.BlockSpec((1, IDX_HEADS, qb_size), lambda b, i: (b, 0, i)),
                  whole((lk, _KVW)), whole((lk // kt, _KVW, kt)), whole((lk, IDX_DIM))],
        out_specs=qrows(_QW),
        out_shape=jax.ShapeDtypeStruct((bsz, lq, _QW), _BF16),
        scratch_shapes=[pltpu.VMEM((lk, qb_size), jnp.int32)],
        compiler_params=_params("arbitrary", "arbitrary"),
    )(q, qi, wi_t, k.astype(_BF16), vt, ki.astype(_BF16))


def _hgrn_lower_bounds(logits):
    p = jax.nn.softmax(logits.astype(_F32), axis=0)
    return jnp.maximum(jnp.cumsum(p, axis=0) - p[:1], 0.0)


def _pad_rows(x, n):
    return jnp.pad(x, ((0, 0), (0, n - x.shape[1]), (0, 0)))


def kernel(x_prompt, x_sample, state_hgrn, cache_k, cache_v, cache_kidx, norm_ffn1, ffn1_w_in, ffn1_w_out, norm_mix, norm_ffn2, ffn2_w_in, ffn2_w_out, hgrn_w_in, hgrn_lb_logits, hgrn_onorm, hgrn_w_out, dsa_w_in, dsa_qnorm, dsa_knorm, dsa_w_out):
    bp, lp, d = x_prompt.shape
    bs, ls, _ = x_sample.shape
    past = cache_k.shape[2]
    depth = norm_ffn1.shape[0]
    assert ls == SUB and lp % 256 == 0
    lb_all = _hgrn_lower_bounds(hgrn_lb_logits)
    tab_p = _rope_tables(jnp.arange(lp, dtype=jnp.int32))
    tab_s = _rope_tables(past + jnp.arange(ls, dtype=jnp.int32))
    lk_s = past + ls
    qb = 128
    lk_pad = -(-lk_s // qb) * qb

    xp = x_prompt.reshape(bp * lp, d)
    xs = x_sample.reshape(bs * ls, d)
    hg_p, hg_s, kp, vp, kip, ksm, vsm, kism = [], [], [], [], [], [], [], []
    for i in range(depth):
        w1 = _ffn_weights(ffn1_w_in[i], ffn1_w_out[i])
        xp = _ffn(xp, norm_ffn1[i], *w1)
        xs = _ffn(xs, norm_ffn1[i], *w1)
        j = i // 2
        if i % 2 == 0:
            w_in = hgrn_w_in[j].astype(_BF16)
            w_out = hgrn_w_out[j].astype(_BF16)
            hp = _linear(xp, w_in, g=norm_mix[i]).reshape(bp, lp, -1)
            hs = _linear(xs, w_in, g=norm_mix[i]).reshape(bs, ls, -1)
            s0 = jnp.zeros((bp,) + state_hgrn.shape[2:], _F32)
            yp, sp = _hgrn_core(hp, s0, lb_all[j], hgrn_onorm[j], tb=256)
            ys, ss = _hgrn_core(hs, state_hgrn[j], lb_all[j], hgrn_onorm[j], tb=ls)
            hg_p.append(sp)
            hg_s.append(ss)
            xp = _linear(yp.reshape(bp * lp, d), w_out, res=xp)
            xs = _linear(ys.reshape(bs * ls, d), w_out, res=xs)
        else:
            w_in = _dsa_in_weight(dsa_w_in[j])
            w_out = dsa_w_out[j].astype(_BF16)
            hp = _linear(xp, w_in, g=norm_mix[i]).reshape(bp, lp, -1)
            hs = _linear(xs, w_in, g=norm_mix[i]).reshape(bs, ls, -1)
            q1, k1, v1, qi1, ki1, wi1 = _dsa_post(hp, dsa_qnorm[j], dsa_knorm[j], tab_p, tb=256)
            q2, k2, v2, qi2, ki2, wi2 = _dsa_post(hs, dsa_qnorm[j], dsa_knorm[j], tab_s, tb=ls)
            op = _dsa_attn(q1, qi1, wi1, k1, v1, ki1, causal=True, lk_true=lp,
                           topk=min(TOPK_MAX, lp // 4), n_valid_q=qb, qb_size=qb, kt=qb)
            k_all = _pad_rows(jnp.concatenate([cache_k[j].reshape(bs, past, _KVW), k2], axis=1), lk_pad)
            v_all = _pad_rows(jnp.concatenate([cache_v[j].reshape(bs, past, _KVW), v2], axis=1), lk_pad)
            ki_all = _pad_rows(jnp.concatenate([cache_kidx[j], ki2], axis=1), lk_pad)
            os_ = _dsa_attn(_pad_rows(q2, qb), _pad_rows(qi2, qb), _pad_rows(wi2, qb), k_all, v_all, ki_all,
                            causal=False, lk_true=lk_s, topk=min(TOPK_MAX, lk_s // 4), n_valid_q=ls,
                            qb_size=qb, kt=qb)[:, :ls]
            kp.append(k1.reshape(bp, lp, ATT_KV_HEADS, ATT_DH))
            vp.append(v1.reshape(bp, lp, ATT_KV_HEADS, ATT_DH))
            kip.append(ki1)
            ksm.append(k2.reshape(bs, ls, ATT_KV_HEADS, ATT_DH))
            vsm.append(v2.reshape(bs, ls, ATT_KV_HEADS, ATT_DH))
            kism.append(ki2)
            xp = _linear(op.reshape(bp * lp, d), w_out, res=xp)
            xs = _linear(os_.reshape(bs * ls, d), w_out, res=xs)
        w2 = _ffn_weights(ffn2_w_in[i], ffn2_w_out[i])
        xp = _ffn(xp, norm_ffn2[i], *w2)
        xs = _ffn(xs, norm_ffn2[i], *w2)
    return (xp.reshape(bp, lp, d), xs.reshape(bs, ls, d), jnp.stack(hg_p),
            jnp.stack(kp), jnp.stack(vp), jnp.stack(kip), jnp.stack(hg_s),
            jnp.stack(ksm), jnp.stack(vsm), jnp.stack(kism))
```

```python
import functools

import numpy as np
import jax
import jax.numpy as jnp
from jax import lax
from jax.experimental import pallas as pl
from jax.experimental.pallas import tpu as pltpu

EPS = 1e-6
CHUNK = 64
HG_HEADS = 8
HG_DK = 128
ATT_HEADS = 8
ATT_KV_HEADS = 2
ATT_GROUP = ATT_HEADS // ATT_KV_HEADS
ATT_DH = 128
IDX_HEADS = 8
IDX_DIM = 64
TOPK_MAX = 256
ROPE_THETA = 10000.0

LANES = 128
SUB = 16
KS = 1024
CS = 512
HPS = 4
NCLS = 256
LOG2E = 1.4426950408889634
VMEM_LIMIT = 56 * 1024 * 1024

INT_MIN = -2 ** 31
KEY_NEG_INF = -2 ** 31 + 0x7FFFFF
NEG_BIG = -1e30
LOGIT_SAFE = 100.0
LOGF_FLOOR = 1e-37

_F32 = jnp.float32
_BF16 = jnp.bfloat16


def _params(*sem):
    return pltpu.CompilerParams(dimension_semantics=sem, vmem_limit_bytes=VMEM_LIMIT)


def _resident(shape):
    nd = len(shape)
    return pl.BlockSpec(shape, lambda *_: (0,) * nd, pipeline_mode=pl.Buffered(1))


def _row_tile(t):
    for tm in (512, 256, 128, 64, 32, 16, 8):
        if t % tm == 0:
            return tm
    raise ValueError(f"unsupported row count {t}")


def _ffn_kernel(x_ref, g_ref, wa_ref, wb_ref, wo_ref, o_ref, acc_ref, *, n_chunks):
    x = x_ref[...]
    ms = jnp.mean(x * x, axis=-1, keepdims=True)
    xn = (x * lax.rsqrt(ms + EPS) * g_ref[...]).astype(_BF16)
    acc_ref[...] = jnp.zeros_like(acc_ref)

    def body(j, carry):
        a = jnp.dot(xn, wa_ref[j], preferred_element_type=_F32)
        b = jnp.dot(xn, wb_ref[j], preferred_element_type=_F32)
        h = (a * jax.nn.sigmoid(a) * b).astype(_BF16)
        acc_ref[...] += jnp.dot(h, wo_ref[j], preferred_element_type=_F32)
        return carry

    lax.fori_loop(0, n_chunks, body, 0)
    o_ref[...] = x + 0.5 * acc_ref[...]


def _ffn_weights(w_in, w_out, tf=256):
    d, two_ff = w_in.shape
    d_ff = two_ff // 2
    n = d_ff // tf
    w = w_in.astype(_BF16).reshape(d, 2, n, tf).transpose(1, 2, 0, 3)
    return w[0], w[1], w_out.astype(_BF16).reshape(n, tf, d)


def _ffn(x, g, wa, wb, wo):
    t, d = x.shape
    n, _, tf = wa.shape
    tm = _row_tile(t)
    return pl.pallas_call(
        functools.partial(_ffn_kernel, n_chunks=n),
        grid=(t // tm,),
        in_specs=[pl.BlockSpec((tm, d), lambda i: (i, 0)),
                  _resident((1, d)), _resident((n, d, tf)), _resident((n, d, tf)),
                  _resident((n, tf, d))],
        out_specs=pl.BlockSpec((tm, d), lambda i: (i, 0)),
        out_shape=jax.ShapeDtypeStruct((t, d), _F32),
        scratch_shapes=[pltpu.VMEM((tm, d), _F32)],
        compiler_params=_params("arbitrary"),
    )(x, g.reshape(1, d), wa, wb, wo)


def _linear_kernel(*refs, has_norm, has_res):
    refs = list(refs)
    x_ref = refs.pop(0)
    g_ref = refs.pop(0) if has_norm else None
    w_ref = refs.pop(0)
    r_ref = refs.pop(0) if has_res else None
    o_ref = refs.pop(0)
    x = x_ref[...]
    if has_norm:
        ms = jnp.mean(x * x, axis=-1, keepdims=True)
        x = x * lax.rsqrt(ms + EPS) * g_ref[...]
    y = jnp.dot(x.astype(_BF16), w_ref[...], preferred_element_type=_F32)
    if has_res:
        y = y + r_ref[...]
    o_ref[...] = y


def _linear(x, w, g=None, res=None):
    t, k = x.shape
    n = w.shape[1]
    tm = _row_tile(t)
    tn = n if n <= 2304 else 1024
    in_specs = [pl.BlockSpec((tm, k), lambda i, j: (i, 0))]
    args = [x]
    if g is not None:
        in_specs.append(_resident((1, k)))
        args.append(g.reshape(1, k))
    if tn == n:
        in_specs.append(_resident((k, n)))
    else:
        in_specs.append(pl.BlockSpec((k, tn), lambda i, j: (0, j)))
    args.append(w)
    if res is not None:
        in_specs.append(pl.BlockSpec((tm, tn), lambda i, j: (i, j)))
        args.append(res)
    return pl.pallas_call(
        functools.partial(_linear_kernel, has_norm=g is not None, has_res=res is not None),
        grid=(t // tm, n // tn),
        in_specs=in_specs,
        out_specs=pl.BlockSpec((tm, tn), lambda i, j: (i, j)),
        out_shape=jax.ShapeDtypeStruct((t, n), _F32),
        compiler_params=_params("arbitrary", "arbitrary"),
    )(*args)


def _hgrn_kernel(q_ref, f_ref, v_ref, g_ref, s0_ref, lb_ref, on_ref, ltri_ref, utri_ref,
                 y_ref, sfin_ref,
                 st_sc, qs_sc, ks_sc, qd_sc, kd_sc, b_sc, dec_sc, o_sc, *, n_sub):
    t = pl.program_id(1)

    @pl.when(t == 0)
    def _init():
        for h in range(HG_HEADS):
            st_sc[h] = s0_ref[0, h].T

    lb = lb_ref[...]
    q = q_ref[0]
    fg = lb + (1.0 - lb) * jax.nn.sigmoid(f_ref[0])
    logf = jnp.log(jnp.maximum(fg, LOGF_FLOOR))
    kk = 1.0 - fg
    qs = q * jax.nn.sigmoid(q)
    bcum = jnp.dot(ltri_ref[...], logf, precision=lax.Precision.HIGHEST,
                   preferred_element_type=_F32)
    brem = jnp.dot(utri_ref[...], logf, precision=lax.Precision.HIGHEST,
                   preferred_element_type=_F32)
    qs_sc[...] = qs
    ks_sc[...] = kk
    b_sc[...] = bcum
    qd_sc[...] = qs * jnp.exp(bcum)
    kd_sc[...] = kk * jnp.exp(brem)
    dec_sc[...] = jnp.exp(bcum + brem)

    ones = jnp.ones((HG_DK, LANES), _BF16)
    tio = lax.broadcasted_iota(jnp.int32, (SUB, 1), 0)

    def sub_body(j, carry):
        r0 = pl.multiple_of(j * SUB, SUB)
        rows = pl.ds(r0, SUB)
        for h in range(HG_HEADS):
            cs = slice(h * HG_DK, (h + 1) * HG_DK)
            st = st_sc[h]
            qd = qd_sc[rows, cs]
            acc = lax.dot_general(qd.astype(_BF16), st.astype(_BF16),
                                  (((1,), (1,)), ((), ())), preferred_element_type=_F32)
            bq = b_sc[rows, cs]
            qr = qs_sc[rows, cs]
            kr = ks_sc[rows, cs]
            vv = v_ref[0, rows, cs]
            terms = []
            for s in range(SUB):
                dm = jnp.where(tio >= s, bq - bq[s:s + 1, :], -jnp.inf)
                terms.append(qr * kr[s:s + 1, :] * jnp.exp(dm))
            tcat = jnp.concatenate(terms, axis=0).astype(_BF16)
            rsum = jnp.dot(tcat, ones, preferred_element_type=_F32)
            for s in range(SUB):
                acc = acc + rsum[s * SUB:(s + 1) * SUB, :] * vv[s:s + 1, :]
            o_sc[rows, cs] = acc
            kd = kd_sc[rows, cs]
            upd = jnp.dot(vv.T.astype(_BF16), kd.astype(_BF16), preferred_element_type=_F32)
            st_sc[h] = dec_sc[pl.ds(r0, 1), cs] * st + upd
        return carry

    lax.fori_loop(0, n_sub, sub_body, 0)

    g = g_ref[0]
    gate = g * jax.nn.sigmoid(g)
    for h in range(HG_HEADS):
        cs = slice(h * HG_DK, (h + 1) * HG_DK)
        oh = o_sc[:, cs]
        ms = jnp.mean(oh * oh, axis=-1, keepdims=True)
        y_ref[0, :, cs] = oh * lax.rsqrt(ms + EPS) * on_ref[...] * gate[:, cs]

    @pl.when(t == pl.num_programs(1) - 1)
    def _fin():
        for h in range(HG_HEADS):
            sfin_ref[0, h] = st_sc[h].T


def _hgrn_core(hproj, s0, lb, onorm, tb):
    bsz, l, w4 = hproj.shape
    w = w4 // 4
    dv = w // HG_HEADS
    assert dv == LANES and HG_DK == LANES and tb % SUB == 0 and l % tb == 0
    idx = np.arange(tb)
    same = (idx[:, None] // SUB) == (idx[None, :] // SUB)
    ltri = jnp.asarray((same & (idx[None, :] <= idx[:, None])).astype(np.float32))
    utri = jnp.asarray((same & (idx[None, :] > idx[:, None])).astype(np.float32))

    def col(c):
        return pl.BlockSpec((1, tb, w), lambda b, t: (b, t, c))

    return pl.pallas_call(
        functools.partial(_hgrn_kernel, n_sub=tb // SUB),
        grid=(bsz, l // tb),
        in_specs=[col(0), col(1), col(2), col(3),
                  pl.BlockSpec((1, HG_HEADS, HG_DK, dv), lambda b, t: (b, 0, 0, 0)),
                  _resident((1, w)), _resident((1, dv)), _resident((tb, tb)), _resident((tb, tb))],
        out_specs=[pl.BlockSpec((1, tb, w), lambda b, t: (b, t, 0)),
                   pl.BlockSpec((1, HG_HEADS, HG_DK, dv), lambda b, t: (b, 0, 0, 0))],
        out_shape=[jax.ShapeDtypeStruct((bsz, l, w), _F32),
                   jax.ShapeDtypeStruct((bsz, HG_HEADS, HG_DK, dv), _F32)],
        scratch_shapes=[pltpu.VMEM((HG_HEADS, dv, HG_DK), _F32)] + [pltpu.VMEM((tb, w), _F32)] * 7,
        compiler_params=_params("arbitrary", "arbitrary"),
    )(hproj, hproj, hproj, hproj, s0, lb.reshape(1, w), onorm.reshape(1, dv), ltri, utri)


_QW = ATT_HEADS * ATT_DH
_KVW = ATT_KV_HEADS * ATT_DH
_QIW = IDX_HEADS * IDX_DIM
_KI_OFF = _QW + 2 * _KVW + _QIW
_WI_OFF = _KI_OFF + LANES
_DSA_PAD = _WI_OFF + LANES


def _dsa_post_kernel(h_ref, qn_ref, kn_ref, c1_ref, s1_ref, c2_ref, s2_ref,
                     q_ref, k_ref, v_ref, qi_ref, ki_ref, wi_ref):
    c1, s1, c2, s2 = c1_ref[...], s1_ref[...], c2_ref[...], s2_ref[...]
    first_half = (lax.broadcasted_iota(jnp.int32, (1, LANES), 1) % IDX_DIM) < (IDX_DIM // 2)

    def norm_rope(x, gain):
        ms = jnp.mean(x * x, axis=-1, keepdims=True)
        x = x * lax.rsqrt(ms + EPS) * gain
        return x * c1 + pltpu.roll(x, ATT_DH // 2, 1) * s1

    def rope_idx(x):
        swapped = jnp.where(first_half, pltpu.roll(x, LANES - IDX_DIM // 2, 1),
                            pltpu.roll(x, IDX_DIM // 2, 1))
        return x * c2 + swapped * s2

    for h in range(ATT_HEADS):
        cs = slice(h * ATT_DH, (h + 1) * ATT_DH)
        q_ref[0, :, cs] = (norm_rope(h_ref[0, :, cs], qn_ref[...]) * (ATT_DH ** -0.5 * LOG2E)).astype(_BF16)
    for n in range(ATT_KV_HEADS):
        cs = slice(n * ATT_DH, (n + 1) * ATT_DH)
        k_ref[0, :, cs] = norm_rope(h_ref[0, :, _QW + n * ATT_DH:_QW + (n + 1) * ATT_DH], kn_ref[...])
    v_ref[0] = h_ref[0, :, _QW + _KVW:_QW + 2 * _KVW]
    for p in range(_QIW // LANES):
        cs = slice(p * LANES, (p + 1) * LANES)
        qi_ref[0, :, cs] = rope_idx(h_ref[0, :, _QW + 2 * _KVW + p * LANES:_QW + 2 * _KVW + (p + 1) * LANES]).astype(_BF16)
    ki_ref[0] = rope_idx(h_ref[0, :, _KI_OFF:_KI_OFF + LANES])[:, :IDX_DIM]
    wi_ref[0] = h_ref[0, :, _WI_OFF:_WI_OFF + IDX_HEADS] * ((IDX_HEADS ** -0.5) * (IDX_DIM ** -0.5))


def _rope_tables(pos):
    def table(d, reps):
        half = d // 2
        inv_freq = jnp.power(ROPE_THETA, -jnp.arange(half, dtype=_F32) * (2.0 / d))
        ang = pos.astype(_F32)[:, None] * inv_freq[None, :]
        c, s = jnp.cos(ang), jnp.sin(ang)
        return (jnp.tile(jnp.concatenate([c, c], axis=-1), (1, reps)),
                jnp.tile(jnp.concatenate([-s, s], axis=-1), (1, reps)))
    c1, s1 = table(ATT_DH, LANES // ATT_DH)
    c2, s2 = table(IDX_DIM, LANES // IDX_DIM)
    return c1, s1, c2, s2


def _dsa_post(hproj, qn, kn, tables, tb):
    bsz, l, _ = hproj.shape

    def rows(wd):
        return pl.BlockSpec((1, tb, wd), lambda b, t: (b, t, 0))

    tab = pl.BlockSpec((tb, LANES), lambda b, t: (t, 0))
    return pl.pallas_call(
        _dsa_post_kernel,
        grid=(bsz, l // tb),
        in_specs=[rows(_DSA_PAD), _resident((1, ATT_DH)), _resident((1, ATT_DH)), tab, tab, tab, tab],
        out_specs=[rows(_QW), rows(_KVW), rows(_KVW), rows(_QIW), rows(IDX_DIM), rows(IDX_HEADS)],
        out_shape=[jax.ShapeDtypeStruct((bsz, l, _QW), _BF16),
                   jax.ShapeDtypeStruct((bsz, l, _KVW), _F32),
                   jax.ShapeDtypeStruct((bsz, l, _KVW), _F32),
                   jax.ShapeDtypeStruct((bsz, l, _QIW), _BF16),
                   jax.ShapeDtypeStruct((bsz, l, IDX_DIM), _F32),
                   jax.ShapeDtypeStruct((bsz, l, IDX_HEADS), _F32)],
        compiler_params=_params("arbitrary", "arbitrary"),
    )(hproj, qn.reshape(1, ATT_DH), kn.reshape(1, ATT_DH), *tables)


def _dsa_in_weight(w_in):
    d = w_in.shape[0]
    z = lambda n: jnp.zeros((d, n), w_in.dtype)
    ki0 = _QW + 2 * _KVW + _QIW
    return jnp.concatenate([w_in[:, :ki0], w_in[:, ki0:ki0 + IDX_DIM], z(LANES - IDX_DIM),
                            w_in[:, ki0 + IDX_DIM:], z(LANES - IDX_HEADS)], axis=1).astype(_BF16)


def _dsa_attn_kernel(bound_ref, q_ref, qi_ref, wi_ref, k_ref, vt_ref, ki_ref, o_ref,
                     key_sc, cmax_sc, acc_sc, *, qb_size, topk, causal, lk_true, n_valid_q):
    qb = pl.program_id(1)
    nq = qb_size
    lane_q = lax.broadcasted_iota(jnp.int32, (1, nq), 1)
    if causal:
        n_steps = ((qb + 1) * nq + KS - 1) // KS
        limit = ((qb * nq + lane_q) // CHUNK + 1) * CHUNK
    else:
        n_steps = k_ref.shape[1] // KS
        limit = jnp.full((1, nq), lk_true, jnp.int32)
    row_io = lax.broadcasted_iota(jnp.int32, (LANES, 1), 0)

    qi_t = qi_ref[0].astype(_F32).T.astype(_BF16)
    qi_pairs = [jnp.concatenate([qi_t[(2 * p + e) * IDX_DIM:(2 * p + e + 1) * IDX_DIM, :]
                                 for e in range(2)], axis=1) for p in range(IDX_HEADS // 2)]
    wi = wi_ref[0]
    cmax_sc[...] = jnp.full(cmax_sc.shape, INT_MIN, jnp.int32)

    def score_body(t, carry):
        for u in range(KS // LANES):
            r0 = pl.multiple_of(t * KS + u * LANES, LANES)
            kit = ki_ref[0, pl.ds(r0, LANES), :]
            acc = jnp.zeros((LANES, nq), _F32)
            for p in range(IDX_HEADS // 2):
                d = jnp.dot(kit, qi_pairs[p], preferred_element_type=_F32)
                for e in range(2):
                    h = 2 * p + e
                    acc = acc + jnp.maximum(d[:, e * nq:(e + 1) * nq], 0.0) * wi[h:h + 1, :]
            acc = jnp.where(r0 + row_io < limit, acc, -jnp.inf)
            bits = pltpu.bitcast(acc, jnp.int32)
            key = bits ^ ((bits >> 31) & 0x7FFFFFFF)
            key_sc[pl.ds(r0, LANES), :] = key
            cls = slice((u % (NCLS // LANES)) * LANES, (u % (NCLS // LANES) + 1) * LANES)
            cmax_sc[cls, :] = jnp.maximum(cmax_sc[cls, :], key)
        return carry

    lax.fori_loop(0, n_steps, score_body, 0)

    def count_ge(cand):
        def body(t, c):
            r0 = pl.multiple_of(t * CS, CS)
            m = (key_sc[pl.ds(r0, CS), :] >= cand).astype(jnp.int32)
            return c + jnp.sum(m.reshape(CS // 8, 8, nq), axis=0)
        c = lax.fori_loop(0, n_steps * (KS // CS), body, jnp.zeros((8, nq), jnp.int32))
        return jnp.sum(c, axis=0, keepdims=True)

    def active(lo, hi, cnt_lo):
        return ((cnt_lo != topk) & (hi - 1 > lo)).astype(jnp.int32)

    cm = cmax_sc[...]
    lo0 = jnp.min(cm, axis=0, keepdims=True)
    top = jnp.max(cm, axis=0, keepdims=True)
    hi0 = jnp.where(top == 2 ** 31 - 1, top, top + 1)
    cnt0 = count_ge(lo0)

    def search_body(st):
        lo, hi, cnt_lo, cnt_hi, _ = st
        cand = (lo >> 1) + (hi >> 1) + (lo & hi & 1)
        c = count_ge(cand)
        act = active(lo, hi, cnt_lo) > 0
        up = act & (c >= topk)
        dn = act & (c < topk)
        lo = jnp.where(up, cand, lo)
        cnt_lo = jnp.where(up, c, cnt_lo)
        hi = jnp.where(dn, cand, hi)
        cnt_hi = jnp.where(dn, c, cnt_hi)
        return lo, hi, cnt_lo, cnt_hi, jnp.max(active(lo, hi, cnt_lo))

    thr, _, n_ge, n_gt, _ = lax.while_loop(
        lambda st: st[4] > 0, search_body,
        (lo0, hi0, cnt0, jnp.zeros((1, nq), jnp.int32), jnp.max(active(lo0, hi0, cnt0))))

    need = (topk - n_gt).astype(_F32)
    over = jnp.max(jnp.where(lane_q < n_valid_q, n_ge, 0)) > topk

    @pl.when(over)
    def _drop_ties():
        rr = lax.broadcasted_iota(jnp.int32, (KS, KS), 0)
        cc = lax.broadcasted_iota(jnp.int32, (KS, KS), 1)
        strict = jnp.where(cc < rr, 1.0, 0.0).astype(_BF16)

        def body(t, run):
            r0 = pl.multiple_of(t * KS, KS)
            kk = key_sc[pl.ds(r0, KS), :]
            tie = kk == thr
            tf = jnp.where(tie, 1.0, 0.0)
            before = jnp.dot(strict, tf.astype(_BF16), preferred_element_type=_F32) + run
            key_sc[pl.ds(r0, KS), :] = jnp.where(tie & (before >= need), KEY_NEG_INF, kk)
            return run + jnp.sum(tf, axis=0, keepdims=True)

        lax.fori_loop(0, n_steps, body, jnp.zeros((1, nq), _F32))

    thr_sel = jnp.maximum(thr, KEY_NEG_INF + 1)

    q_t = q_ref[0].astype(_F32).T.astype(_BF16)
    n_pair = ATT_GROUP // HPS
    rhs = [[jnp.concatenate([q_t[(n * ATT_GROUP + HPS * pr + e) * ATT_DH:(n * ATT_GROUP + HPS * pr + e + 1) * ATT_DH, :]
                             for e in range(HPS)], axis=1) for pr in range(n_pair)]
           for n in range(ATT_KV_HEADS)]
    n_slot = ATT_KV_HEADS * n_pair

    def step_bias(t):
        r0 = pl.multiple_of(t * KS, KS)
        bias = jnp.where(key_sc[pl.ds(r0, KS), :] >= thr_sel, 0.0, NEG_BIG).astype(_BF16)
        return r0, jnp.concatenate([bias] * HPS, axis=1)

    def write_out():
        for n in range(ATT_KV_HEADS):
            for pr in range(n_pair):
                a = acc_sc[n * n_pair + pr]
                o_t = a[:ATT_DH, :] / a[ATT_DH:ATT_DH + 1, :]
                for e in range(HPS):
                    hh = n * ATT_GROUP + HPS * pr + e
                    o_ref[0, :, hh * ATT_DH:(hh + 1) * ATT_DH] = o_t[:, e * nq:(e + 1) * nq].T.astype(_BF16)

    fast = bound_ref[0] <= LOGIT_SAFE

    @pl.when(fast)
    def _attend_unshifted():
        acc_sc[...] = jnp.zeros_like(acc_sc)

        def body(t, carry):
            r0, bias2 = step_bias(t)
            for n in range(ATT_KV_HEADS):
                k_rows = k_ref[0, pl.ds(r0, KS), n * ATT_DH:(n + 1) * ATT_DH]
                for pr in range(n_pair):
                    slot = n * n_pair + pr
                    s = jnp.dot(k_rows, rhs[n][pr], preferred_element_type=_F32)
                    p = jnp.exp2(s.astype(_BF16) + bias2)
                    acc_sc[slot] += jnp.dot(vt_ref[0, t, n], p, preferred_element_type=_F32)
            return carry

        lax.fori_loop(0, n_steps, body, 0)
        write_out()

    @pl.when(jnp.logical_not(fast))
    def _attend_two_pass():
        def max_body(t, ms):
            r0, bias2 = step_bias(t)
            out = []
            for n in range(ATT_KV_HEADS):
                k_rows = k_ref[0, pl.ds(r0, KS), n * ATT_DH:(n + 1) * ATT_DH]
                for pr in range(n_pair):
                    s = jnp.dot(k_rows, rhs[n][pr], preferred_element_type=_F32).astype(_BF16) + bias2
                    out.append(jnp.maximum(ms[n * n_pair + pr],
                                           jnp.max(s, axis=0, keepdims=True).astype(_F32)))
            return tuple(out)

        ms = lax.fori_loop(0, n_steps, max_body,
                           tuple(jnp.full((1, HPS * nq), NEG_BIG, _F32) for _ in range(n_slot)))

        first_row = lax.broadcasted_iota(jnp.int32, (ATT_DH, 1), 0) == 0
        ones_col = jnp.where(lax.broadcasted_iota(jnp.int32, (KS, ATT_DH), 1) == 0, 1.0, 0.0).astype(_BF16)
        rhs_aug = []
        for n in range(ATT_KV_HEADS):
            for pr in range(n_pair):
                m_ref = jnp.where(ms[n * n_pair + pr] > 0.5 * NEG_BIG, ms[n * n_pair + pr], 0.0)
                shift = jnp.where(first_row, -m_ref, 0.0).astype(_BF16)
                rhs_aug.append(jnp.concatenate([rhs[n][pr], shift], axis=0))
        acc_sc[...] = jnp.zeros_like(acc_sc)

        def pv_body(t, carry):
            r0, bias2 = step_bias(t)
            for n in range(ATT_KV_HEADS):
                k_aug = jnp.concatenate([k_ref[0, pl.ds(r0, KS), n * ATT_DH:(n + 1) * ATT_DH], ones_col], axis=1)
                for pr in range(n_pair):
                    slot = n * n_pair + pr
                    s = jnp.dot(k_aug, rhs_aug[slot], preferred_element_type=_F32)
                    p = jnp.exp2(s.astype(_BF16) + bias2)
                    acc_sc[slot] += jnp.dot(vt_ref[0, t, n], p, preferred_element_type=_F32)
            return carry

        lax.fori_loop(0, n_steps, pv_body, 0)
        write_out()


def _dsa_attn(q, qi, wi, k, v, ki, logit_bound, *, causal, lk_true, topk, n_valid_q, qb_size=128):
    bsz, lq, _ = q.shape
    lk = k.shape[1]
    ns = lk // KS
    assert lq % qb_size == 0 and lk % KS == 0 and KS % NCLS == 0 and topk <= NCLS
    wi_t = jnp.swapaxes(wi, 1, 2)
    vt = v.astype(_BF16).reshape(bsz, ns, KS, ATT_KV_HEADS, ATT_DH).transpose(0, 1, 3, 4, 2)
    ones_rows = jnp.zeros((bsz, ns, ATT_KV_HEADS, 8, KS), _BF16).at[:, :, :, 0, :].set(1.0)
    vt = jnp.concatenate([vt, ones_rows], axis=3)

    def whole(shape):
        nd = len(shape)
        return pl.BlockSpec((1,) + shape, lambda b, i: (b,) + (0,) * nd, pipeline_mode=pl.Buffered(1))

    def qrows(wd):
        return pl.BlockSpec((1, qb_size, wd), lambda b, i: (b, i, 0))

    return pl.pallas_call(
        functools.partial(_dsa_attn_kernel, qb_size=qb_size, topk=topk, causal=causal,
                          lk_true=lk_true, n_valid_q=n_valid_q),
        grid=(bsz, lq // qb_size),
        in_specs=[pl.BlockSpec(memory_space=pltpu.SMEM), qrows(_QW), qrows(_QIW),
                  pl.BlockSpec((1, IDX_HEADS, qb_size), lambda b, i: (b, 0, i)),
                  whole((lk, _KVW)), whole((ns, ATT_KV_HEADS, ATT_DH + 8, KS)), whole((lk, IDX_DIM))],
        out_specs=qrows(_QW),
        out_shape=jax.ShapeDtypeStruct((bsz, lq, _QW), _BF16),
        scratch_shapes=[pltpu.VMEM((lk, qb_size), jnp.int32),
                        pltpu.VMEM((NCLS, qb_size), jnp.int32),
                        pltpu.VMEM((ATT_KV_HEADS * ATT_GROUP // HPS, ATT_DH + 8, HPS * qb_size), _F32)],
        compiler_params=_params("arbitrary", "arbitrary"),
    )(jnp.reshape(logit_bound, (1,)).astype(_F32), q, qi, wi_t, k.astype(_BF16), vt, ki.astype(_BF16))


def _hgrn_lower_bounds(logits):
    p = jax.nn.softmax(logits.astype(_F32), axis=0)
    return jnp.maximum(jnp.cumsum(p, axis=0) - p[:1], 0.0)


def _pad_rows(x, n):
    return jnp.pad(x, ((0, 0), (0, n - x.shape[1]), (0, 0)))


def kernel(x_prompt, x_sample, state_hgrn, cache_k, cache_v, cache_kidx, norm_ffn1, ffn1_w_in, ffn1_w_out, norm_mix, norm_ffn2, ffn2_w_in, ffn2_w_out, hgrn_w_in, hgrn_lb_logits, hgrn_onorm, hgrn_w_out, dsa_w_in, dsa_qnorm, dsa_knorm, dsa_w_out):
    bp, lp, d = x_prompt.shape
    bs, ls, _ = x_sample.shape
    past = cache_k.shape[2]
    depth = norm_ffn1.shape[0]
    assert ls == SUB and lp % 256 == 0
    lb_all = _hgrn_lower_bounds(hgrn_lb_logits)
    tab_p = _rope_tables(jnp.arange(lp, dtype=jnp.int32))
    tab_s = _rope_tables(past + jnp.arange(ls, dtype=jnp.int32))
    lk_s = past + ls
    qb = 128
    lk_pad = -(-lk_s // KS) * KS

    xp = x_prompt.reshape(bp * lp, d)
    xs = x_sample.reshape(bs * ls, d)
    hg_p, hg_s, kp, vp, kip, ksm, vsm, kism = [], [], [], [], [], [], [], []
    for i in range(depth):
        w1 = _ffn_weights(ffn1_w_in[i], ffn1_w_out[i])
        xp = _ffn(xp, norm_ffn1[i], *w1)
        xs = _ffn(xs, norm_ffn1[i], *w1)
        j = i // 2
        if i % 2 == 0:
            w_in = hgrn_w_in[j].astype(_BF16)
            w_out = hgrn_w_out[j].astype(_BF16)
            hp = _linear(xp, w_in, g=norm_mix[i]).reshape(bp, lp, -1)
            hs = _linear(xs, w_in, g=norm_mix[i]).reshape(bs, ls, -1)
            s0 = jnp.zeros((bp,) + state_hgrn.shape[2:], _F32)
            yp, sp = _hgrn_core(hp, s0, lb_all[j], hgrn_onorm[j], tb=256)
            ys, ss = _hgrn_core(hs, state_hgrn[j], lb_all[j], hgrn_onorm[j], tb=ls)
            hg_p.append(sp)
            hg_s.append(ss)
            xp = _linear(yp.reshape(bp * lp, d), w_out, res=xp)
            xs = _linear(ys.reshape(bs * ls, d), w_out, res=xs)
        else:
            w_in = _dsa_in_weight(dsa_w_in[j])
            w_out = dsa_w_out[j].astype(_BF16)
            hp = _linear(xp, w_in, g=norm_mix[i]).reshape(bp, lp, -1)
            hs = _linear(xs, w_in, g=norm_mix[i]).reshape(bs, ls, -1)
            q1, k1, v1, qi1, ki1, wi1 = _dsa_post(hp, dsa_qnorm[j], dsa_knorm[j], tab_p, tb=256)
            q2, k2, v2, qi2, ki2, wi2 = _dsa_post(hs, dsa_qnorm[j], dsa_knorm[j], tab_s, tb=ls)
            bound = (1.01 * ATT_DH ** 0.5 * LOG2E) * jnp.max(jnp.abs(dsa_qnorm[j])) * jnp.max(jnp.abs(dsa_knorm[j]))
            op = _dsa_attn(q1, qi1, wi1, k1, v1, ki1, bound, causal=True, lk_true=lp,
                           topk=min(TOPK_MAX, lp // 4), n_valid_q=qb, qb_size=qb)
            k_all = _pad_rows(jnp.concatenate([cache_k[j].reshape(bs, past, _KVW), k2], axis=1), lk_pad)
            v_all = _pad_rows(jnp.concatenate([cache_v[j].reshape(bs, past, _KVW), v2], axis=1), lk_pad)
            ki_all = _pad_rows(jnp.concatenate([cache_kidx[j], ki2], axis=1), lk_pad)
            os_ = _dsa_attn(_pad_rows(q2, qb), _pad_rows(qi2, qb), _pad_rows(wi2, qb), k_all, v_all, ki_all,
                            jnp.inf, causal=False, lk_true=lk_s, topk=min(TOPK_MAX, lk_s // 4), n_valid_q=ls,
                            qb_size=qb)[:, :ls]
            kp.append(k1.reshape(bp, lp, ATT_KV_HEADS, ATT_DH))
            vp.append(v1.reshape(bp, lp, ATT_KV_HEADS, ATT_DH))
            kip.append(ki1)
            ksm.append(k2.reshape(bs, ls, ATT_KV_HEADS, ATT_DH))
            vsm.append(v2.reshape(bs, ls, ATT_KV_HEADS, ATT_DH))
            kism.append(ki2)
            xp = _linear(op.reshape(bp * lp, d), w_out, res=xp)
            xs = _linear(os_.reshape(bs * ls, d), w_out, res=xs)
        w2 = _ffn_weights(ffn2_w_in[i], ffn2_w_out[i])
        xp = _ffn(xp, norm_ffn2[i], *w2)
        xs = _ffn(xs, norm_ffn2[i], *w2)
    return (xp.reshape(bp, lp, d), xs.reshape(bs, ls, d), jnp.stack(hg_p),
            jnp.stack(kp), jnp.stack(vp), jnp.stack(kip), jnp.stack(hg_s),
            jnp.stack(ksm), jnp.stack(vsm), jnp.stack(kism))
```

```python
import functools

import numpy as np
import jax
import jax.numpy as jnp
from jax import lax
from jax.experimental import pallas as pl
from jax.experimental.pallas import tpu as pltpu

EPS = 1e-6
CHUNK = 64
HG_HEADS = 8
HG_DK = 128
ATT_HEADS = 8
ATT_KV_HEADS = 2
ATT_GROUP = ATT_HEADS // ATT_KV_HEADS
ATT_DH = 128
IDX_HEADS = 8
IDX_DIM = 64
TOPK_MAX = 256
ROPE_THETA = 10000.0

LANES = 128
SUB = 16
KS = 1024
CS = 1024
HPS = 4
PACK = 16
I16_MIN = -2 ** 15
I16_MAX = 2 ** 15 - 1
LOG2E = 1.4426950408889634
VMEM_LIMIT = 56 * 1024 * 1024

INT_MIN = -2 ** 31
KEY_NEG_INF = -2 ** 31 + 0x7FFFFF
NEG_BIG = -1e30
LOGIT_SAFE = 100.0
LOGF_FLOOR = 1e-37

_F32 = jnp.float32
_BF16 = jnp.bfloat16


def _params(*sem):
    return pltpu.CompilerParams(dimension_semantics=sem, vmem_limit_bytes=VMEM_LIMIT)


def _resident(shape):
    nd = len(shape)
    return pl.BlockSpec(shape, lambda *_: (0,) * nd, pipeline_mode=pl.Buffered(1))


def _row_tile(t):
    for tm in (512, 256, 128, 64, 32, 16, 8):
        if t % tm == 0:
            return tm
    raise ValueError(f"unsupported row count {t}")


def _ffn_kernel(x_ref, g_ref, wa_ref, wb_ref, wo_ref, o_ref, acc_ref, *, n_chunks):
    x = x_ref[...]
    ms = jnp.mean(x * x, axis=-1, keepdims=True)
    xn = (x * lax.rsqrt(ms + EPS) * g_ref[...]).astype(_BF16)
    acc_ref[...] = jnp.zeros_like(acc_ref)

    def body(j, carry):
        a = jnp.dot(xn, wa_ref[j], preferred_element_type=_F32)
        b = jnp.dot(xn, wb_ref[j], preferred_element_type=_F32)
        h = (a * jax.nn.sigmoid(a) * b).astype(_BF16)
        acc_ref[...] += jnp.dot(h, wo_ref[j], preferred_element_type=_F32)
        return carry

    lax.fori_loop(0, n_chunks, body, 0)
    o_ref[...] = x + 0.5 * acc_ref[...]


def _ffn_weights(w_in, w_out, tf=256):
    d, two_ff = w_in.shape
    d_ff = two_ff // 2
    n = d_ff // tf
    w = w_in.astype(_BF16).reshape(d, 2, n, tf).transpose(1, 2, 0, 3)
    return w[0], w[1], w_out.astype(_BF16).reshape(n, tf, d)


def _ffn(x, g, wa, wb, wo):
    t, d = x.shape
    n, _, tf = wa.shape
    tm = _row_tile(t)
    return pl.pallas_call(
        functools.partial(_ffn_kernel, n_chunks=n),
        grid=(t // tm,),
        in_specs=[pl.BlockSpec((tm, d), lambda i: (i, 0)),
                  _resident((1, d)), _resident((n, d, tf)), _resident((n, d, tf)),
                  _resident((n, tf, d))],
        out_specs=pl.BlockSpec((tm, d), lambda i: (i, 0)),
        out_shape=jax.ShapeDtypeStruct((t, d), _F32),
        scratch_shapes=[pltpu.VMEM((tm, d), _F32)],
        compiler_params=_params("arbitrary"),
    )(x, g.reshape(1, d), wa, wb, wo)


def _linear_kernel(*refs, has_norm, has_res):
    refs = list(refs)
    x_ref = refs.pop(0)
    g_ref = refs.pop(0) if has_norm else None
    w_ref = refs.pop(0)
    r_ref = refs.pop(0) if has_res else None
    o_ref = refs.pop(0)
    x = x_ref[...]
    if has_norm:
        ms = jnp.mean(x * x, axis=-1, keepdims=True)
        x = x * lax.rsqrt(ms + EPS) * g_ref[...]
    y = jnp.dot(x.astype(_BF16), w_ref[...], preferred_element_type=_F32)
    if has_res:
        y = y + r_ref[...]
    o_ref[...] = y


def _linear(x, w, g=None, res=None):
    t, k = x.shape
    n = w.shape[1]
    tm = _row_tile(t)
    tn = n if n <= 2304 else 1024
    in_specs = [pl.BlockSpec((tm, k), lambda i, j: (i, 0))]
    args = [x]
    if g is not None:
        in_specs.append(_resident((1, k)))
        args.append(g.reshape(1, k))
    if tn == n:
        in_specs.append(_resident((k, n)))
    else:
        in_specs.append(pl.BlockSpec((k, tn), lambda i, j: (0, j)))
    args.append(w)
    if res is not None:
        in_specs.append(pl.BlockSpec((tm, tn), lambda i, j: (i, j)))
        args.append(res)
    return pl.pallas_call(
        functools.partial(_linear_kernel, has_norm=g is not None, has_res=res is not None),
        grid=(t // tm, n // tn),
        in_specs=in_specs,
        out_specs=pl.BlockSpec((tm, tn), lambda i, j: (i, j)),
        out_shape=jax.ShapeDtypeStruct((t, n), _F32),
        compiler_params=_params("arbitrary", "arbitrary"),
    )(*args)


def _hgrn_kernel(q_ref, f_ref, v_ref, g_ref, s0_ref, lb_ref, on_ref, ltri_ref, utri_ref,
                 y_ref, sfin_ref,
                 st_sc, qs_sc, ks_sc, qd_sc, kd_sc, b_sc, dec_sc, o_sc, *, n_sub):
    t = pl.program_id(1)

    @pl.when(t == 0)
    def _init():
        for h in range(HG_HEADS):
            st_sc[h] = s0_ref[0, h].T

    lb = lb_ref[...]
    q = q_ref[0]
    fg = lb + (1.0 - lb) * jax.nn.sigmoid(f_ref[0])
    logf = jnp.log(jnp.maximum(fg, LOGF_FLOOR))
    kk = 1.0 - fg
    qs = q * jax.nn.sigmoid(q)
    bcum = jnp.dot(ltri_ref[...], logf, precision=lax.Precision.HIGHEST,
                   preferred_element_type=_F32)
    brem = jnp.dot(utri_ref[...], logf, precision=lax.Precision.HIGHEST,
                   preferred_element_type=_F32)
    qs_sc[...] = qs
    ks_sc[...] = kk
    b_sc[...] = bcum
    qd_sc[...] = qs * jnp.exp(bcum)
    kd_sc[...] = kk * jnp.exp(brem)
    dec_sc[...] = jnp.exp(bcum + brem)

    ones = jnp.ones((HG_DK, LANES), _BF16)
    tio = lax.broadcasted_iota(jnp.int32, (SUB, 1), 0)

    def sub_body(j, carry):
        r0 = pl.multiple_of(j * SUB, SUB)
        rows = pl.ds(r0, SUB)
        for h in range(HG_HEADS):
            cs = slice(h * HG_DK, (h + 1) * HG_DK)
            st = st_sc[h]
            qd = qd_sc[rows, cs]
            acc = lax.dot_general(qd.astype(_BF16), st.astype(_BF16),
                                  (((1,), (1,)), ((), ())), preferred_element_type=_F32)
            bq = b_sc[rows, cs]
            qr = qs_sc[rows, cs]
            kr = ks_sc[rows, cs]
            vv = v_ref[0, rows, cs]
            terms = []
            for s in range(SUB):
                dm = jnp.where(tio >= s, bq - bq[s:s + 1, :], -jnp.inf)
                terms.append(qr * kr[s:s + 1, :] * jnp.exp(dm))
            tcat = jnp.concatenate(terms, axis=0).astype(_BF16)
            rsum = jnp.dot(tcat, ones, preferred_element_type=_F32)
            for s in range(SUB):
                acc = acc + rsum[s * SUB:(s + 1) * SUB, :] * vv[s:s + 1, :]
            o_sc[rows, cs] = acc
            kd = kd_sc[rows, cs]
            upd = jnp.dot(vv.T.astype(_BF16), kd.astype(_BF16), preferred_element_type=_F32)
            st_sc[h] = dec_sc[pl.ds(r0, 1), cs] * st + upd
        return carry

    lax.fori_loop(0, n_sub, sub_body, 0)

    g = g_ref[0]
    gate = g * jax.nn.sigmoid(g)
    for h in range(HG_HEADS):
        cs = slice(h * HG_DK, (h + 1) * HG_DK)
        oh = o_sc[:, cs]
        ms = jnp.mean(oh * oh, axis=-1, keepdims=True)
        y_ref[0, :, cs] = oh * lax.rsqrt(ms + EPS) * on_ref[...] * gate[:, cs]

    @pl.when(t == pl.num_programs(1) - 1)
    def _fin():
        for h in range(HG_HEADS):
            sfin_ref[0, h] = st_sc[h].T


def _hgrn_core(hproj, s0, lb, onorm, tb):
    bsz, l, w4 = hproj.shape
    w = w4 // 4
    dv = w // HG_HEADS
    assert dv == LANES and HG_DK == LANES and tb % SUB == 0 and l % tb == 0
    idx = np.arange(tb)
    same = (idx[:, None] // SUB) == (idx[None, :] // SUB)
    ltri = jnp.asarray((same & (idx[None, :] <= idx[:, None])).astype(np.float32))
    utri = jnp.asarray((same & (idx[None, :] > idx[:, None])).astype(np.float32))

    def col(c):
        return pl.BlockSpec((1, tb, w), lambda b, t: (b, t, c))

    return pl.pallas_call(
        functools.partial(_hgrn_kernel, n_sub=tb // SUB),
        grid=(bsz, l // tb),
        in_specs=[col(0), col(1), col(2), col(3),
                  pl.BlockSpec((1, HG_HEADS, HG_DK, dv), lambda b, t: (b, 0, 0, 0)),
                  _resident((1, w)), _resident((1, dv)), _resident((tb, tb)), _resident((tb, tb))],
        out_specs=[pl.BlockSpec((1, tb, w), lambda b, t: (b, t, 0)),
                   pl.BlockSpec((1, HG_HEADS, HG_DK, dv), lambda b, t: (b, 0, 0, 0))],
        out_shape=[jax.ShapeDtypeStruct((bsz, l, w), _F32),
                   jax.ShapeDtypeStruct((bsz, HG_HEADS, HG_DK, dv), _F32)],
        scratch_shapes=[pltpu.VMEM((HG_HEADS, dv, HG_DK), _F32)] + [pltpu.VMEM((tb, w), _F32)] * 7,
        compiler_params=_params("arbitrary", "arbitrary"),
    )(hproj, hproj, hproj, hproj, s0, lb.reshape(1, w), onorm.reshape(1, dv), ltri, utri)


_QW = ATT_HEADS * ATT_DH
_KVW = ATT_KV_HEADS * ATT_DH
_QIW = IDX_HEADS * IDX_DIM
_KI_OFF = _QW + 2 * _KVW + _QIW
_WI_OFF = _KI_OFF + LANES
_DSA_PAD = _WI_OFF + LANES


def _dsa_post_kernel(h_ref, qn_ref, kn_ref, c1_ref, s1_ref, c2_ref, s2_ref,
                     q_ref, k_ref, v_ref, qi_ref, ki_ref, wi_ref):
    c1, s1, c2, s2 = c1_ref[...], s1_ref[...], c2_ref[...], s2_ref[...]
    first_half = (lax.broadcasted_iota(jnp.int32, (1, LANES), 1) % IDX_DIM) < (IDX_DIM // 2)

    def norm_rope(x, gain):
        ms = jnp.mean(x * x, axis=-1, keepdims=True)
        x = x * lax.rsqrt(ms + EPS) * gain
        return x * c1 + pltpu.roll(x, ATT_DH // 2, 1) * s1

    def rope_idx(x):
        swapped = jnp.where(first_half, pltpu.roll(x, LANES - IDX_DIM // 2, 1),
                            pltpu.roll(x, IDX_DIM // 2, 1))
        return x * c2 + swapped * s2

    for h in range(ATT_HEADS):
        cs = slice(h * ATT_DH, (h + 1) * ATT_DH)
        q_ref[0, :, cs] = (norm_rope(h_ref[0, :, cs], qn_ref[...]) * (ATT_DH ** -0.5 * LOG2E)).astype(_BF16)
    for n in range(ATT_KV_HEADS):
        cs = slice(n * ATT_DH, (n + 1) * ATT_DH)
        k_ref[0, :, cs] = norm_rope(h_ref[0, :, _QW + n * ATT_DH:_QW + (n + 1) * ATT_DH], kn_ref[...])
    v_ref[0] = h_ref[0, :, _QW + _KVW:_QW + 2 * _KVW]
    for p in range(_QIW // LANES):
        cs = slice(p * LANES, (p + 1) * LANES)
        qi_ref[0, :, cs] = rope_idx(h_ref[0, :, _QW + 2 * _KVW + p * LANES:_QW + 2 * _KVW + (p + 1) * LANES]).astype(_BF16)
    ki_ref[0] = rope_idx(h_ref[0, :, _KI_OFF:_KI_OFF + LANES])[:, :IDX_DIM]
    wi_ref[0] = h_ref[0, :, _WI_OFF:_WI_OFF + IDX_HEADS] * ((IDX_HEADS ** -0.5) * (IDX_DIM ** -0.5))


def _rope_tables(pos):
    def table(d, reps):
        half = d // 2
        inv_freq = jnp.power(ROPE_THETA, -jnp.arange(half, dtype=_F32) * (2.0 / d))
        ang = pos.astype(_F32)[:, None] * inv_freq[None, :]
        c, s = jnp.cos(ang), jnp.sin(ang)
        return (jnp.tile(jnp.concatenate([c, c], axis=-1), (1, reps)),
                jnp.tile(jnp.concatenate([-s, s], axis=-1), (1, reps)))
    c1, s1 = table(ATT_DH, LANES // ATT_DH)
    c2, s2 = table(IDX_DIM, LANES // IDX_DIM)
    return c1, s1, c2, s2


def _dsa_post(hproj, qn, kn, tables, tb):
    bsz, l, _ = hproj.shape

    def rows(wd):
        return pl.BlockSpec((1, tb, wd), lambda b, t: (b, t, 0))

    tab = pl.BlockSpec((tb, LANES), lambda b, t: (t, 0))
    return pl.pallas_call(
        _dsa_post_kernel,
        grid=(bsz, l // tb),
        in_specs=[rows(_DSA_PAD), _resident((1, ATT_DH)), _resident((1, ATT_DH)), tab, tab, tab, tab],
        out_specs=[rows(_QW), rows(_KVW), rows(_KVW), rows(_QIW), rows(IDX_DIM), rows(IDX_HEADS)],
        out_shape=[jax.ShapeDtypeStruct((bsz, l, _QW), _BF16),
                   jax.ShapeDtypeStruct((bsz, l, _KVW), _F32),
                   jax.ShapeDtypeStruct((bsz, l, _KVW), _F32),
                   jax.ShapeDtypeStruct((bsz, l, _QIW), _BF16),
                   jax.ShapeDtypeStruct((bsz, l, IDX_DIM), _F32),
                   jax.ShapeDtypeStruct((bsz, l, IDX_HEADS), _F32)],
        compiler_params=_params("arbitrary", "arbitrary"),
    )(hproj, qn.reshape(1, ATT_DH), kn.reshape(1, ATT_DH), *tables)


def _dsa_in_weight(w_in):
    d = w_in.shape[0]
    z = lambda n: jnp.zeros((d, n), w_in.dtype)
    ki0 = _QW + 2 * _KVW + _QIW
    return jnp.concatenate([w_in[:, :ki0], w_in[:, ki0:ki0 + IDX_DIM], z(LANES - IDX_DIM),
                            w_in[:, ki0 + IDX_DIM:], z(LANES - IDX_HEADS)], axis=1).astype(_BF16)


def _dsa_attn_kernel(bound_ref, q_ref, qi_ref, wi_ref, k_ref, vt_ref, ki_ref, o_ref,
                     key_sc, hi_sc, lo_sc, acc_sc, *, qb_size, topk, causal, lk_true, n_valid_q):
    qb = pl.program_id(1)
    nq = qb_size
    lane_q = lax.broadcasted_iota(jnp.int32, (1, nq), 1)
    if causal:
        n_steps = ((qb + 1) * nq + KS - 1) // KS
        limit = ((qb * nq + lane_q) // CHUNK + 1) * CHUNK
    else:
        n_steps = k_ref.shape[1] // KS
        limit = jnp.full((1, nq), lk_true, jnp.int32)
    row_io = lax.broadcasted_iota(jnp.int32, (LANES, 1), 0)

    qi_t = qi_ref[0].astype(_F32).T.astype(_BF16)
    qi_pairs = [jnp.concatenate([qi_t[(2 * p + e) * IDX_DIM:(2 * p + e + 1) * IDX_DIM, :]
                                 for e in range(2)], axis=1) for p in range(IDX_HEADS // 2)]
    wi = wi_ref[0]

    def score_body(t, carry):
        for u in range(KS // LANES):
            r0 = pl.multiple_of(t * KS + u * LANES, LANES)
            kit = ki_ref[0, pl.ds(r0, LANES), :]
            acc = jnp.zeros((LANES, nq), _F32)
            for p in range(IDX_HEADS // 2):
                d = jnp.dot(kit, qi_pairs[p], preferred_element_type=_F32)
                for e in range(2):
                    h = 2 * p + e
                    acc = acc + jnp.maximum(d[:, e * nq:(e + 1) * nq], 0.0) * wi[h:h + 1, :]
            acc = jnp.where(r0 + row_io < limit, acc, -jnp.inf)
            bits = pltpu.bitcast(acc, jnp.int32)
            key = bits ^ ((bits >> 31) & 0x7FFFFFFF)
            key_sc[pl.ds(r0, LANES), :] = key
            hi_sc[pl.ds(r0, LANES), :] = (key >> 16).astype(jnp.int16)
            lo_sc[pl.ds(r0, LANES), :] = ((key & 0xFFFF) + I16_MIN).astype(jnp.int16)
        return carry

    lax.fori_loop(0, n_steps, score_body, 0)

    def count16(ref, cand):
        c16 = cand.astype(jnp.int16)
        one = jnp.ones((), jnp.int16)
        zero = jnp.zeros((), jnp.int16)

        def body(t, c):
            r0 = pl.multiple_of(t * CS, CS)
            m = jnp.where(ref[pl.ds(r0, CS), :] >= c16, one, zero)
            parts = [m[i * PACK:(i + 1) * PACK, :] for i in range(CS // PACK)]
            while len(parts) > 1:
                parts = [parts[i] + parts[i + 1] for i in range(0, len(parts), 2)]
            return c + parts[0]

        c = lax.fori_loop(0, n_steps * (KS // CS), body, jnp.zeros((PACK, nq), jnp.int16))
        return jnp.sum(c.astype(jnp.int32), axis=0, keepdims=True)

    def bit_search(ref, base, cnt_all):
        c = base + count16(ref, jnp.zeros((1, nq), jnp.int32))
        thr = jnp.where(c >= topk, 0, I16_MIN).astype(jnp.int32)
        cnt = jnp.where(c >= topk, c, cnt_all)

        def body(i, st):
            thr, cnt = st
            cand = thr | lax.shift_left(jnp.int32(1), 14 - i)
            c = base + count16(ref, cand)
            return jnp.where(c >= topk, cand, thr), jnp.where(c >= topk, c, cnt)

        return lax.fori_loop(0, 15, body, (thr, cnt))

    zero_row = jnp.zeros((1, nq), jnp.int32)
    th, ge_h = bit_search(hi_sc, zero_row, zero_row + n_steps * KS)
    gt_h = jnp.where(th == I16_MAX, 0, count16(hi_sc, jnp.minimum(th + 1, I16_MAX)))
    th16 = th.astype(jnp.int16)

    def bucket_body(t, carry):
        r0 = pl.multiple_of(t * CS, CS)
        lo_sc[pl.ds(r0, CS), :] = jnp.where(hi_sc[pl.ds(r0, CS), :] == th16, lo_sc[pl.ds(r0, CS), :],
                                            jnp.full((), I16_MIN, jnp.int16))
        return carry

    lax.fori_loop(0, n_steps * (KS // CS), bucket_body, 0)
    tl, n_ge = bit_search(lo_sc, gt_h, ge_h)
    thr = (th << 16) + (tl - I16_MIN)

    over = jnp.max(jnp.where(lane_q < n_valid_q, n_ge, 0)) > topk

    @pl.when(over)
    def _drop_ties():
        rr = lax.broadcasted_iota(jnp.int32, (KS, KS), 0)
        cc = lax.broadcasted_iota(jnp.int32, (KS, KS), 1)
        strict = jnp.where(cc < rr, 1.0, 0.0).astype(_BF16)

        def gt_body(t, c):
            r0 = pl.multiple_of(t * KS, KS)
            return c + jnp.sum(jnp.where(key_sc[pl.ds(r0, KS), :] > thr, 1.0, 0.0), axis=0, keepdims=True)

        need = topk - lax.fori_loop(0, n_steps, gt_body, jnp.zeros((1, nq), _F32))

        def body(t, run):
            r0 = pl.multiple_of(t * KS, KS)
            kk = key_sc[pl.ds(r0, KS), :]
            tie = kk == thr
            tf = jnp.where(tie, 1.0, 0.0)
            before = jnp.dot(strict, tf.astype(_BF16), preferred_element_type=_F32) + run
            key_sc[pl.ds(r0, KS), :] = jnp.where(tie & (before >= need), KEY_NEG_INF, kk)
            return run + jnp.sum(tf, axis=0, keepdims=True)

        lax.fori_loop(0, n_steps, body, jnp.zeros((1, nq), _F32))

    thr_sel = jnp.maximum(thr, KEY_NEG_INF + 1)

    q_t = q_ref[0].astype(_F32).T.astype(_BF16)
    n_pair = ATT_GROUP // HPS
    rhs = [[jnp.concatenate([q_t[(n * ATT_GROUP + HPS * pr + e) * ATT_DH:(n * ATT_GROUP + HPS * pr + e + 1) * ATT_DH, :]
                             for e in range(HPS)], axis=1) for pr in range(n_pair)]
           for n in range(ATT_KV_HEADS)]
    n_slot = ATT_KV_HEADS * n_pair

    def step_bias(t):
        r0 = pl.multiple_of(t * KS, KS)
        bias = jnp.where(key_sc[pl.ds(r0, KS), :] >= thr_sel, 0.0, NEG_BIG).astype(_BF16)
        return r0, jnp.concatenate([bias] * HPS, axis=1)

    def write_out():
        for n in range(ATT_KV_HEADS):
            for pr in range(n_pair):
                a = acc_sc[n * n_pair + pr]
                o_t = a[:ATT_DH, :] / a[ATT_DH:ATT_DH + 1, :]
                for e in range(HPS):
                    hh = n * ATT_GROUP + HPS * pr + e
                    o_ref[0, :, hh * ATT_DH:(hh + 1) * ATT_DH] = o_t[:, e * nq:(e + 1) * nq].T.astype(_BF16)

    fast = bound_ref[0] <= LOGIT_SAFE

    @pl.when(fast)
    def _attend_unshifted():
        acc_sc[...] = jnp.zeros_like(acc_sc)

        def body(t, carry):
            r0, bias2 = step_bias(t)
            for n in range(ATT_KV_HEADS):
                k_rows = k_ref[0, pl.ds(r0, KS), n * ATT_DH:(n + 1) * ATT_DH]
                for pr in range(n_pair):
                    slot = n * n_pair + pr
                    s = jnp.dot(k_rows, rhs[n][pr], preferred_element_type=_F32)
                    p = jnp.exp2(s.astype(_BF16) + bias2)
                    acc_sc[slot] += jnp.dot(vt_ref[0, t, n], p, preferred_element_type=_F32)
            return carry

        lax.fori_loop(0, n_steps, body, 0)
        write_out()

    @pl.when(jnp.logical_not(fast))
    def _attend_two_pass():
        def max_body(t, ms):
            r0, bias2 = step_bias(t)
            out = []
            for n in range(ATT_KV_HEADS):
                k_rows = k_ref[0, pl.ds(r0, KS), n * ATT_DH:(n + 1) * ATT_DH]
                for pr in range(n_pair):
                    s = jnp.dot(k_rows, rhs[n][pr], preferred_element_type=_F32).astype(_BF16) + bias2
                    out.append(jnp.maximum(ms[n * n_pair + pr],
                                           jnp.max(s, axis=0, keepdims=True).astype(_F32)))
            return tuple(out)

        ms = lax.fori_loop(0, n_steps, max_body,
                           tuple(jnp.full((1, HPS * nq), NEG_BIG, _F32) for _ in range(n_slot)))

        first_row = lax.broadcasted_iota(jnp.int32, (ATT_DH, 1), 0) == 0
        ones_col = jnp.where(lax.broadcasted_iota(jnp.int32, (KS, ATT_DH), 1) == 0, 1.0, 0.0).astype(_BF16)
        rhs_aug = []
        for n in range(ATT_KV_HEADS):
            for pr in range(n_pair):
                m_ref = jnp.where(ms[n * n_pair + pr] > 0.5 * NEG_BIG, ms[n * n_pair + pr], 0.0)
                shift = jnp.where(first_row, -m_ref, 0.0).astype(_BF16)
                rhs_aug.append(jnp.concatenate([rhs[n][pr], shift], axis=0))
        acc_sc[...] = jnp.zeros_like(acc_sc)

        def pv_body(t, carry):
            r0, bias2 = step_bias(t)
            for n in range(ATT_KV_HEADS):
                k_aug = jnp.concatenate([k_ref[0, pl.ds(r0, KS), n * ATT_DH:(n + 1) * ATT_DH], ones_col], axis=1)
                for pr in range(n_pair):
                    slot = n * n_pair + pr
                    s = jnp.dot(k_aug, rhs_aug[slot], preferred_element_type=_F32)
                    p = jnp.exp2(s.astype(_BF16) + bias2)
                    acc_sc[slot] += jnp.dot(vt_ref[0, t, n], p, preferred_element_type=_F32)
            return carry

        lax.fori_loop(0, n_steps, pv_body, 0)
        write_out()


def _dsa_attn(q, qi, wi, k, v, ki, logit_bound, *, causal, lk_true, topk, n_valid_q, qb_size=128):
    bsz, lq, _ = q.shape
    lk = k.shape[1]
    ns = lk // KS
    assert lq % qb_size == 0 and lk % KS == 0 and KS % CS == 0
    assert lk // PACK <= I16_MAX
    wi_t = jnp.swapaxes(wi, 1, 2)
    vt = v.astype(_BF16).reshape(bsz, ns, KS, ATT_KV_HEADS, ATT_DH).transpose(0, 1, 3, 4, 2)
    ones_rows = jnp.zeros((bsz, ns, ATT_KV_HEADS, 8, KS), _BF16).at[:, :, :, 0, :].set(1.0)
    vt = jnp.concatenate([vt, ones_rows], axis=3)

    def whole(shape):
        nd = len(shape)
        return pl.BlockSpec((1,) + shape, lambda b, i: (b,) + (0,) * nd, pipeline_mode=pl.Buffered(1))

    def qrows(wd):
        return pl.BlockSpec((1, qb_size, wd), lambda b, i: (b, i, 0))

    return pl.pallas_call(
        functools.partial(_dsa_attn_kernel, qb_size=qb_size, topk=topk, causal=causal,
                          lk_true=lk_true, n_valid_q=n_valid_q),
        grid=(bsz, lq // qb_size),
        in_specs=[pl.BlockSpec(memory_space=pltpu.SMEM), qrows(_QW), qrows(_QIW),
                  pl.BlockSpec((1, IDX_HEADS, qb_size), lambda b, i: (b, 0, i)),
                  whole((lk, _KVW)), whole((ns, ATT_KV_HEADS, ATT_DH + 8, KS)), whole((lk, IDX_DIM))],
        out_specs=qrows(_QW),
        out_shape=jax.ShapeDtypeStruct((bsz, lq, _QW), _BF16),
        scratch_shapes=[pltpu.VMEM((lk, qb_size), jnp.int32),
                        pltpu.VMEM((lk, qb_size), jnp.int16),
                        pltpu.VMEM((lk, qb_size), jnp.int16),
                        pltpu.VMEM((ATT_KV_HEADS * ATT_GROUP // HPS, ATT_DH + 8, HPS * qb_size), _F32)],
        compiler_params=_params("arbitrary", "arbitrary"),
    )(jnp.reshape(logit_bound, (1,)).astype(_F32), q, qi, wi_t, k.astype(_BF16), vt, ki.astype(_BF16))


def _hgrn_lower_bounds(logits):
    p = jax.nn.softmax(logits.astype(_F32), axis=0)
    return jnp.maximum(jnp.cumsum(p, axis=0) - p[:1], 0.0)


def _pad_rows(x, n):
    return jnp.pad(x, ((0, 0), (0, n - x.shape[1]), (0, 0)))


def kernel(x_prompt, x_sample, state_hgrn, cache_k, cache_v, cache_kidx, norm_ffn1, ffn1_w_in, ffn1_w_out, norm_mix, norm_ffn2, ffn2_w_in, ffn2_w_out, hgrn_w_in, hgrn_lb_logits, hgrn_onorm, hgrn_w_out, dsa_w_in, dsa_qnorm, dsa_knorm, dsa_w_out):
    bp, lp, d = x_prompt.shape
    bs, ls, _ = x_sample.shape
    past = cache_k.shape[2]
    depth = norm_ffn1.shape[0]
    assert ls == SUB and lp % 256 == 0
    lb_all = _hgrn_lower_bounds(hgrn_lb_logits)
    tab_p = _rope_tables(jnp.arange(lp, dtype=jnp.int32))
    tab_s = _rope_tables(past + jnp.arange(ls, dtype=jnp.int32))
    lk_s = past + ls
    qb = 128
    lk_pad = -(-lk_s // KS) * KS

    xp = x_prompt.reshape(bp * lp, d)
    xs = x_sample.reshape(bs * ls, d)
    hg_p, hg_s, kp, vp, kip, ksm, vsm, kism = [], [], [], [], [], [], [], []
    for i in range(depth):
        w1 = _ffn_weights(ffn1_w_in[i], ffn1_w_out[i])
        xp = _ffn(xp, norm_ffn1[i], *w1)
        xs = _ffn(xs, norm_ffn1[i], *w1)
        j = i // 2
        if i % 2 == 0:
            w_in = hgrn_w_in[j].astype(_BF16)
            w_out = hgrn_w_out[j].astype(_BF16)
            hp = _linear(xp, w_in, g=norm_mix[i]).reshape(bp, lp, -1)
            hs = _linear(xs, w_in, g=norm_mix[i]).reshape(bs, ls, -1)
            s0 = jnp.zeros((bp,) + state_hgrn.shape[2:], _F32)
            yp, sp = _hgrn_core(hp, s0, lb_all[j], hgrn_onorm[j], tb=256)
            ys, ss = _hgrn_core(hs, state_hgrn[j], lb_all[j], hgrn_onorm[j], tb=ls)
            hg_p.append(sp)
            hg_s.append(ss)
            xp = _linear(yp.reshape(bp * lp, d), w_out, res=xp)
            xs = _linear(ys.reshape(bs * ls, d), w_out, res=xs)
        else:
            w_in = _dsa_in_weight(dsa_w_in[j])
            w_out = dsa_w_out[j].astype(_BF16)
            hp = _linear(xp, w_in, g=norm_mix[i]).reshape(bp, lp, -1)
            hs = _linear(xs, w_in, g=norm_mix[i]).reshape(bs, ls, -1)
            q1, k1, v1, qi1, ki1, wi1 = _dsa_post(hp, dsa_qnorm[j], dsa_knorm[j], tab_p, tb=256)
            q2, k2, v2, qi2, ki2, wi2 = _dsa_post(hs, dsa_qnorm[j], dsa_knorm[j], tab_s, tb=ls)
            bound = (1.01 * ATT_DH ** 0.5 * LOG2E) * jnp.max(jnp.abs(dsa_qnorm[j])) * jnp.max(jnp.abs(dsa_knorm[j]))
            op = _dsa_attn(q1, qi1, wi1, k1, v1, ki1, bound, causal=True, lk_true=lp,
                           topk=min(TOPK_MAX, lp // 4), n_valid_q=qb, qb_size=qb)
            k_all = _pad_rows(jnp.concatenate([cache_k[j].reshape(bs, past, _KVW), k2], axis=1), lk_pad)
            v_all = _pad_rows(jnp.concatenate([cache_v[j].reshape(bs, past, _KVW), v2], axis=1), lk_pad)
            ki_all = _pad_rows(jnp.concatenate([cache_kidx[j], ki2], axis=1), lk_pad)
            os_ = _dsa_attn(_pad_rows(q2, qb), _pad_rows(qi2, qb), _pad_rows(wi2, qb), k_all, v_all, ki_all,
                            jnp.inf, causal=False, lk_true=lk_s, topk=min(TOPK_MAX, lk_s // 4), n_valid_q=ls,
                            qb_size=qb)[:, :ls]
            kp.append(k1.reshape(bp, lp, ATT_KV_HEADS, ATT_DH))
            vp.append(v1.reshape(bp, lp, ATT_KV_HEADS, ATT_DH))
            kip.append(ki1)
            ksm.append(k2.reshape(bs, ls, ATT_KV_HEADS, ATT_DH))
            vsm.append(v2.reshape(bs, ls, ATT_KV_HEADS, ATT_DH))
            kism.append(ki2)
            xp = _linear(op.reshape(bp * lp, d), w_out, res=xp)
            xs = _linear(os_.reshape(bs * ls, d), w_out, res=xs)
        w2 = _ffn_weights(ffn2_w_in[i], ffn2_w_out[i])
        xp = _ffn(xp, norm_ffn2[i], *w2)
        xs = _ffn(xs, norm_ffn2[i], *w2)
    return (xp.reshape(bp, lp, d), xs.reshape(bs, ls, d), jnp.stack(hg_p),
            jnp.stack(kp), jnp.stack(vp), jnp.stack(kip), jnp.stack(hg_s),
            jnp.stack(ksm), jnp.stack(vsm), jnp.stack(kism))
```

```python
import functools

import numpy as np
import jax
import jax.numpy as jnp
from jax import lax
from jax.experimental import pallas as pl
from jax.experimental.pallas import tpu as pltpu

EPS = 1e-6
CHUNK = 64
HG_HEADS = 8
HG_DK = 128
ATT_HEADS = 8
ATT_KV_HEADS = 2
ATT_GROUP = ATT_HEADS // ATT_KV_HEADS
ATT_DH = 128
IDX_HEADS = 8
IDX_DIM = 64
TOPK_MAX = 256
ROPE_THETA = 10000.0

LANES = 128
SUB = 16
HALF = SUB // 2
KS = 1024
CS = 512
HPS = 4
NCLS = 256
SEARCH_UNROLL = 2
LOG2E = 1.4426950408889634
VMEM_LIMIT = 56 * 1024 * 1024

INT_MIN = -2 ** 31
KEY_NEG_INF = -2 ** 31 + 0x7FFFFF
NEG_BIG = -1e30
LOGIT_SAFE = 100.0
LOGF_FLOOR = 1e-37

_F32 = jnp.float32
_BF16 = jnp.bfloat16


def _params(*sem):
    return pltpu.CompilerParams(dimension_semantics=sem, vmem_limit_bytes=VMEM_LIMIT)


def _resident(shape):
    nd = len(shape)
    return pl.BlockSpec(shape, lambda *_: (0,) * nd, pipeline_mode=pl.Buffered(1))


def _row_tile(t, cap=512):
    for tm in (1024, 512, 256, 128, 64, 32, 16, 8):
        if t % tm == 0 and tm <= cap:
            return tm
    raise ValueError(f"unsupported row count {t}")


def _ffn_kernel(x_ref, g_ref, wa_ref, wb_ref, wo_ref, o_ref, acc_ref, *, n_chunks):
    x = x_ref[...]
    ms = jnp.mean(x * x, axis=-1, keepdims=True)
    xn = (x * lax.rsqrt(ms + EPS) * g_ref[...]).astype(_BF16)
    acc_ref[...] = jnp.zeros_like(acc_ref)

    def body(j, carry):
        a = jnp.dot(xn, wa_ref[j], preferred_element_type=_F32)
        b = jnp.dot(xn, wb_ref[j], preferred_element_type=_F32)
        h = (a * jax.nn.sigmoid(a) * b).astype(_BF16)
        acc_ref[...] += jnp.dot(h, wo_ref[j], preferred_element_type=_F32)
        return carry

    lax.fori_loop(0, n_chunks, body, 0)
    o_ref[...] = x + 0.5 * acc_ref[...]


def _ffn_weights(w_in, w_out, tf=256):
    d, two_ff = w_in.shape
    d_ff = two_ff // 2
    n = d_ff // tf
    w = w_in.astype(_BF16).reshape(d, 2, n, tf).transpose(1, 2, 0, 3)
    return w[0], w[1], w_out.astype(_BF16).reshape(n, tf, d)


def _ffn(x, g, wa, wb, wo):
    t, d = x.shape
    n, _, tf = wa.shape
    tm = _row_tile(t, cap=1024)
    return pl.pallas_call(
        functools.partial(_ffn_kernel, n_chunks=n),
        grid=(t // tm,),
        in_specs=[pl.BlockSpec((tm, d), lambda i: (i, 0)),
                  _resident((1, d)), _resident((n, d, tf)), _resident((n, d, tf)),
                  _resident((n, tf, d))],
        out_specs=pl.BlockSpec((tm, d), lambda i: (i, 0)),
        out_shape=jax.ShapeDtypeStruct((t, d), _F32),
        scratch_shapes=[pltpu.VMEM((tm, d), _F32)],
        compiler_params=_params("arbitrary"),
    )(x, g.reshape(1, d), wa, wb, wo)


def _linear_kernel(*refs, has_norm, has_res):
    refs = list(refs)
    x_ref = refs.pop(0)
    g_ref = refs.pop(0) if has_norm else None
    w_ref = refs.pop(0)
    r_ref = refs.pop(0) if has_res else None
    o_ref = refs.pop(0)
    x = x_ref[...]
    if has_norm:
        ms = jnp.mean(x * x, axis=-1, keepdims=True)
        x = x * lax.rsqrt(ms + EPS) * g_ref[...]
    y = jnp.dot(x.astype(_BF16), w_ref[...], preferred_element_type=_F32)
    if has_res:
        y = y + r_ref[...]
    o_ref[...] = y


def _linear(x, w, g=None, res=None):
    t, k = x.shape
    n = w.shape[1]
    tm = _row_tile(t)
    tn = n if n <= 2304 else 1024
    in_specs = [pl.BlockSpec((tm, k), lambda i, j: (i, 0))]
    args = [x]
    if g is not None:
        in_specs.append(_resident((1, k)))
        args.append(g.reshape(1, k))
    if tn == n:
        in_specs.append(_resident((k, n)))
    else:
        in_specs.append(pl.BlockSpec((k, tn), lambda i, j: (0, j)))
    args.append(w)
    if res is not None:
        in_specs.append(pl.BlockSpec((tm, tn), lambda i, j: (i, j)))
        args.append(res)
    return pl.pallas_call(
        functools.partial(_linear_kernel, has_norm=g is not None, has_res=res is not None),
        grid=(t // tm, n // tn),
        in_specs=in_specs,
        out_specs=pl.BlockSpec((tm, tn), lambda i, j: (i, j)),
        out_shape=jax.ShapeDtypeStruct((t, n), _F32),
        compiler_params=_params("arbitrary", "arbitrary"),
    )(*args)


def _hgrn_kernel(q_ref, f_ref, v_ref, g_ref, s0_ref, lb_ref, on_ref, ltri_ref, utri_ref,
                 y_ref, sfin_ref,
                 st_sc, qs_sc, ks_sc, qd_sc, kd_sc, b_sc, dec_sc, o_sc, *, n_sub):
    t = pl.program_id(1)

    @pl.when(t == 0)
    def _init():
        for h in range(HG_HEADS):
            st_sc[h] = s0_ref[0, h].T

    lb = lb_ref[...]
    q = q_ref[0]
    fg = lb + (1.0 - lb) * jax.nn.sigmoid(f_ref[0])
    logf = jnp.log(jnp.maximum(fg, LOGF_FLOOR))
    kk = 1.0 - fg
    qs = q * jax.nn.sigmoid(q)
    logf_hi = logf.astype(_BF16)
    logf_lo = (logf - logf_hi.astype(_F32)).astype(_BF16)

    def segsum(tri_ref):
        return (jnp.dot(tri_ref[...], logf_hi, preferred_element_type=_F32)
                + jnp.dot(tri_ref[...], logf_lo, preferred_element_type=_F32))

    bcum = segsum(ltri_ref)
    brem = segsum(utri_ref)
    qs_sc[...] = qs
    ks_sc[...] = kk
    b_sc[...] = bcum
    qd_sc[...] = qs * jnp.exp(bcum)
    kd_sc[...] = kk * jnp.exp(brem)
    dec_sc[...] = jnp.exp(bcum + brem)

    ones = jnp.ones((HG_DK, LANES), _BF16)
    tio = lax.broadcasted_iota(jnp.int32, (SUB, 1), 0)

    def sub_body(j, carry):
        r0 = pl.multiple_of(j * SUB, SUB)
        rows = pl.ds(r0, SUB)
        for h in range(HG_HEADS):
            cs = slice(h * HG_DK, (h + 1) * HG_DK)
            st = st_sc[h]
            qd = qd_sc[rows, cs]
            acc = lax.dot_general(qd.astype(_BF16), st.astype(_BF16),
                                  (((1,), (1,)), ((), ())), preferred_element_type=_F32)
            bq = b_sc[rows, cs]
            qr = qs_sc[rows, cs]
            kr = ks_sc[rows, cs]
            vv = v_ref[0, rows, cs]
            terms = []
            for s in range(SUB):
                t0 = 0 if s < HALF else HALF
                dm = jnp.where(tio[t0:] >= s, bq[t0:] - bq[s:s + 1, :], -jnp.inf)
                terms.append(qr[t0:] * kr[s:s + 1, :] * jnp.exp(dm))
            tcat = jnp.concatenate(terms, axis=0).astype(_BF16)
            rsum = jnp.dot(tcat, ones, preferred_element_type=_F32)
            top, bot, off = acc[:HALF], acc[HALF:], 0
            for s in range(SUB):
                if s < HALF:
                    top = top + rsum[off:off + HALF, :] * vv[s:s + 1, :]
                    off += HALF
                bot = bot + rsum[off:off + HALF, :] * vv[s:s + 1, :]
                off += HALF
            o_sc[rows, cs] = jnp.concatenate([top, bot], axis=0)
            kd = kd_sc[rows, cs]
            upd = jnp.dot(vv.T.astype(_BF16), kd.astype(_BF16), preferred_element_type=_F32)
            st_sc[h] = dec_sc[pl.ds(r0, 1), cs] * st + upd
        return carry

    lax.fori_loop(0, n_sub, sub_body, 0)

    g = g_ref[0]
    gate = g * jax.nn.sigmoid(g)
    for h in range(HG_HEADS):
        cs = slice(h * HG_DK, (h + 1) * HG_DK)
        oh = o_sc[:, cs]
        ms = jnp.mean(oh * oh, axis=-1, keepdims=True)
        y_ref[0, :, cs] = oh * lax.rsqrt(ms + EPS) * on_ref[...] * gate[:, cs]

    @pl.when(t == pl.num_programs(1) - 1)
    def _fin():
        for h in range(HG_HEADS):
            sfin_ref[0, h] = st_sc[h].T


def _hgrn_core(hproj, s0, lb, onorm, tb):
    bsz, l, w4 = hproj.shape
    w = w4 // 4
    dv = w // HG_HEADS
    assert dv == LANES and HG_DK == LANES and tb % SUB == 0 and l % tb == 0
    idx = np.arange(tb)
    same = (idx[:, None] // SUB) == (idx[None, :] // SUB)
    ltri = jnp.asarray((same & (idx[None, :] <= idx[:, None])).astype(np.float32)).astype(_BF16)
    utri = jnp.asarray((same & (idx[None, :] > idx[:, None])).astype(np.float32)).astype(_BF16)

    def col(c):
        return pl.BlockSpec((1, tb, w), lambda b, t: (b, t, c))

    return pl.pallas_call(
        functools.partial(_hgrn_kernel, n_sub=tb // SUB),
        grid=(bsz, l // tb),
        in_specs=[col(0), col(1), col(2), col(3),
                  pl.BlockSpec((1, HG_HEADS, HG_DK, dv), lambda b, t: (b, 0, 0, 0)),
                  _resident((1, w)), _resident((1, dv)), _resident((tb, tb)), _resident((tb, tb))],
        out_specs=[pl.BlockSpec((1, tb, w), lambda b, t: (b, t, 0)),
                   pl.BlockSpec((1, HG_HEADS, HG_DK, dv), lambda b, t: (b, 0, 0, 0))],
        out_shape=[jax.ShapeDtypeStruct((bsz, l, w), _F32),
                   jax.ShapeDtypeStruct((bsz, HG_HEADS, HG_DK, dv), _F32)],
        scratch_shapes=[pltpu.VMEM((HG_HEADS, dv, HG_DK), _F32)] + [pltpu.VMEM((tb, w), _F32)] * 7,
        compiler_params=_params("arbitrary", "arbitrary"),
    )(hproj, hproj, hproj, hproj, s0, lb.reshape(1, w), onorm.reshape(1, dv), ltri, utri)


_QW = ATT_HEADS * ATT_DH
_KVW = ATT_KV_HEADS * ATT_DH
_QIW = IDX_HEADS * IDX_DIM
_KI_OFF = _QW + 2 * _KVW + _QIW
_WI_OFF = _KI_OFF + LANES
_DSA_PAD = _WI_OFF + LANES


def _dsa_post_kernel(h_ref, qn_ref, kn_ref, c1_ref, s1_ref, c2_ref, s2_ref,
                     q_ref, k_ref, v_ref, qi_ref, ki_ref, wi_ref):
    c1, s1, c2, s2 = c1_ref[...], s1_ref[...], c2_ref[...], s2_ref[...]
    first_half = (lax.broadcasted_iota(jnp.int32, (1, LANES), 1) % IDX_DIM) < (IDX_DIM // 2)

    def norm_rope(x, gain):
        ms = jnp.mean(x * x, axis=-1, keepdims=True)
        x = x * lax.rsqrt(ms + EPS) * gain
        return x * c1 + pltpu.roll(x, ATT_DH // 2, 1) * s1

    def rope_idx(x):
        swapped = jnp.where(first_half, pltpu.roll(x, LANES - IDX_DIM // 2, 1),
                            pltpu.roll(x, IDX_DIM // 2, 1))
        return x * c2 + swapped * s2

    for h in range(ATT_HEADS):
        cs = slice(h * ATT_DH, (h + 1) * ATT_DH)
        q_ref[0, :, cs] = (norm_rope(h_ref[0, :, cs], qn_ref[...]) * (ATT_DH ** -0.5 * LOG2E)).astype(_BF16)
    for n in range(ATT_KV_HEADS):
        cs = slice(n * ATT_DH, (n + 1) * ATT_DH)
        k_ref[0, :, cs] = norm_rope(h_ref[0, :, _QW + n * ATT_DH:_QW + (n + 1) * ATT_DH], kn_ref[...])
    v_ref[0] = h_ref[0, :, _QW + _KVW:_QW + 2 * _KVW]
    for p in range(_QIW // LANES):
        cs = slice(p * LANES, (p + 1) * LANES)
        qi_ref[0, :, cs] = rope_idx(h_ref[0, :, _QW + 2 * _KVW + p * LANES:_QW + 2 * _KVW + (p + 1) * LANES]).astype(_BF16)
    ki_ref[0] = rope_idx(h_ref[0, :, _KI_OFF:_KI_OFF + LANES])[:, :IDX_DIM]
    wi_ref[0] = h_ref[0, :, _WI_OFF:_WI_OFF + IDX_HEADS] * ((IDX_HEADS ** -0.5) * (IDX_DIM ** -0.5))


def _rope_tables(pos):
    def table(d, reps):
        half = d // 2
        inv_freq = jnp.power(ROPE_THETA, -jnp.arange(half, dtype=_F32) * (2.0 / d))
        ang = pos.astype(_F32)[:, None] * inv_freq[None, :]
        c, s = jnp.cos(ang), jnp.sin(ang)
        return (jnp.tile(jnp.concatenate([c, c], axis=-1), (1, reps)),
                jnp.tile(jnp.concatenate([-s, s], axis=-1), (1, reps)))
    c1, s1 = table(ATT_DH, LANES // ATT_DH)
    c2, s2 = table(IDX_DIM, LANES // IDX_DIM)
    return c1, s1, c2, s2


def _dsa_post(hproj, qn, kn, tables, tb):
    bsz, l, _ = hproj.shape

    def rows(wd):
        return pl.BlockSpec((1, tb, wd), lambda b, t: (b, t, 0))

    tab = pl.BlockSpec((tb, LANES), lambda b, t: (t, 0))
    return pl.pallas_call(
        _dsa_post_kernel,
        grid=(bsz, l // tb),
        in_specs=[rows(_DSA_PAD), _resident((1, ATT_DH)), _resident((1, ATT_DH)), tab, tab, tab, tab],
        out_specs=[rows(_QW), rows(_KVW), rows(_KVW), rows(_QIW), rows(IDX_DIM), rows(IDX_HEADS)],
        out_shape=[jax.ShapeDtypeStruct((bsz, l, _QW), _BF16),
                   jax.ShapeDtypeStruct((bsz, l, _KVW), _F32),
                   jax.ShapeDtypeStruct((bsz, l, _KVW), _F32),
                   jax.ShapeDtypeStruct((bsz, l, _QIW), _BF16),
                   jax.ShapeDtypeStruct((bsz, l, IDX_DIM), _F32),
                   jax.ShapeDtypeStruct((bsz, l, IDX_HEADS), _F32)],
        compiler_params=_params("arbitrary", "arbitrary"),
    )(hproj, qn.reshape(1, ATT_DH), kn.reshape(1, ATT_DH), *tables)


def _dsa_in_weight(w_in):
    d = w_in.shape[0]
    z = lambda n: jnp.zeros((d, n), w_in.dtype)
    ki0 = _QW + 2 * _KVW + _QIW
    return jnp.concatenate([w_in[:, :ki0], w_in[:, ki0:ki0 + IDX_DIM], z(LANES - IDX_DIM),
                            w_in[:, ki0 + IDX_DIM:], z(LANES - IDX_HEADS)], axis=1).astype(_BF16)


def _dsa_attn_kernel(bound_ref, q_ref, qi_ref, wi_ref, k_ref, vt_ref, ki_ref, o_ref,
                     key_sc, cmax_sc, acc_sc, *, qb_size, topk, causal, lk_true, n_valid_q):
    qb = pl.program_id(1)
    nq = qb_size
    lane_q = lax.broadcasted_iota(jnp.int32, (1, nq), 1)
    if causal:
        n_steps = ((qb + 1) * nq + KS - 1) // KS
        limit = ((qb * nq + lane_q) // CHUNK + 1) * CHUNK
    else:
        n_steps = k_ref.shape[1] // KS
        limit = jnp.full((1, nq), lk_true, jnp.int32)
    row_io = lax.broadcasted_iota(jnp.int32, (LANES, 1), 0)

    qi_t = qi_ref[0].astype(_F32).T.astype(_BF16)
    qi_pairs = [jnp.concatenate([qi_t[(2 * p + e) * IDX_DIM:(2 * p + e + 1) * IDX_DIM, :]
                                 for e in range(2)], axis=1) for p in range(IDX_HEADS // 2)]
    wi = wi_ref[0]
    cmax_sc[...] = jnp.full(cmax_sc.shape, INT_MIN, jnp.int32)

    def score_body(t, carry):
        for u in range(KS // LANES):
            r0 = pl.multiple_of(t * KS + u * LANES, LANES)
            kit = ki_ref[0, pl.ds(r0, LANES), :]
            acc = jnp.zeros((LANES, nq), _F32)
            for p in range(IDX_HEADS // 2):
                d = jnp.dot(kit, qi_pairs[p], preferred_element_type=_F32)
                for e in range(2):
                    h = 2 * p + e
                    acc = acc + jnp.maximum(d[:, e * nq:(e + 1) * nq], 0.0) * wi[h:h + 1, :]
            acc = jnp.where(r0 + row_io < limit, acc, -jnp.inf)
            bits = pltpu.bitcast(acc, jnp.int32)
            key = bits ^ ((bits >> 31) & 0x7FFFFFFF)
            key_sc[pl.ds(r0, LANES), :] = key
            cls = slice((u % (NCLS // LANES)) * LANES, (u % (NCLS // LANES) + 1) * LANES)
            cmax_sc[cls, :] = jnp.maximum(cmax_sc[cls, :], key)
        return carry

    lax.fori_loop(0, n_steps, score_body, 0)

    def count_ge(cand):
        def body(t, c):
            r0 = pl.multiple_of(t * CS, CS)
            m = (key_sc[pl.ds(r0, CS), :] >= cand).astype(jnp.int32)
            return c + jnp.sum(m.reshape(CS // 8, 8, nq), axis=0)
        c = lax.fori_loop(0, n_steps * (KS // CS), body, jnp.zeros((8, nq), jnp.int32))
        return jnp.sum(c, axis=0, keepdims=True)

    def active(lo, hi, cnt_lo):
        return jnp.where((cnt_lo != topk) & (hi - 1 > lo), 1.0, 0.0)

    cm = cmax_sc[...]
    lo0 = jnp.min(cm, axis=0, keepdims=True)
    top = jnp.max(cm, axis=0, keepdims=True)
    hi0 = jnp.where(top == 2 ** 31 - 1, top, top + 1)
    cnt0 = count_ge(lo0)

    def halve(st):
        lo, hi, cnt_lo, cnt_hi = st
        cand = (lo >> 1) + (hi >> 1) + (lo & hi & 1)
        c = count_ge(cand)
        act = active(lo, hi, cnt_lo) > 0
        up = act & (c >= topk)
        dn = act & (c < topk)
        return (jnp.where(up, cand, lo), jnp.where(dn, cand, hi),
                jnp.where(up, c, cnt_lo), jnp.where(dn, c, cnt_hi))

    def search_body(st):
        st = st[:4]
        for _ in range(SEARCH_UNROLL):
            st = halve(st)
        return st + (jnp.max(active(st[0], st[1], st[2])),)

    thr, _, n_ge, n_gt, _ = lax.while_loop(
        lambda st: st[4] > 0, search_body,
        (lo0, hi0, cnt0, jnp.zeros((1, nq), jnp.int32), jnp.max(active(lo0, hi0, cnt0))))

    need = (topk - n_gt).astype(_F32)
    over = jnp.max(jnp.where(lane_q < n_valid_q, n_ge, 0)) > topk

    @pl.when(over)
    def _drop_ties():
        rr = lax.broadcasted_iota(jnp.int32, (KS, KS), 0)
        cc = lax.broadcasted_iota(jnp.int32, (KS, KS), 1)
        strict = jnp.where(cc < rr, 1.0, 0.0).astype(_BF16)

        def body(t, run):
            r0 = pl.multiple_of(t * KS, KS)
            kk = key_sc[pl.ds(r0, KS), :]
            tie = kk == thr
            tf = jnp.where(tie, 1.0, 0.0)
            before = jnp.dot(strict, tf.astype(_BF16), preferred_element_type=_F32) + run
            key_sc[pl.ds(r0, KS), :] = jnp.where(tie & (before >= need), KEY_NEG_INF, kk)
            return run + jnp.sum(tf, axis=0, keepdims=True)

        lax.fori_loop(0, n_steps, body, jnp.zeros((1, nq), _F32))

    thr_sel = jnp.maximum(thr, KEY_NEG_INF + 1)

    q_t = q_ref[0].astype(_F32).T.astype(_BF16)
    n_pair = ATT_GROUP // HPS
    rhs = [[jnp.concatenate([q_t[(n * ATT_GROUP + HPS * pr + e) * ATT_DH:(n * ATT_GROUP + HPS * pr + e + 1) * ATT_DH, :]
                             for e in range(HPS)], axis=1) for pr in range(n_pair)]
           for n in range(ATT_KV_HEADS)]
    n_slot = ATT_KV_HEADS * n_pair

    def step_bias(t):
        r0 = pl.multiple_of(t * KS, KS)
        bias = jnp.where(key_sc[pl.ds(r0, KS), :] >= thr_sel, 0.0, NEG_BIG).astype(_BF16)
        return r0, jnp.concatenate([bias] * HPS, axis=1)

    def write_out():
        for n in range(ATT_KV_HEADS):
            for pr in range(n_pair):
                a = acc_sc[n * n_pair + pr]
                o_t = a[:ATT_DH, :] / a[ATT_DH:ATT_DH + 1, :]
                for e in range(HPS):
                    hh = n * ATT_GROUP + HPS * pr + e
                    o_ref[0, :, hh * ATT_DH:(hh + 1) * ATT_DH] = o_t[:, e * nq:(e + 1) * nq].T.astype(_BF16)

    fast = bound_ref[0] <= LOGIT_SAFE

    @pl.when(fast)
    def _attend_unshifted():
        acc_sc[...] = jnp.zeros_like(acc_sc)

        def body(t, carry):
            r0, bias2 = step_bias(t)
            for n in range(ATT_KV_HEADS):
                k_rows = k_ref[0, pl.ds(r0, KS), n * ATT_DH:(n + 1) * ATT_DH]
                for pr in range(n_pair):
                    slot = n * n_pair + pr
                    s = jnp.dot(k_rows, rhs[n][pr], preferred_element_type=_F32)
                    p = jnp.exp2(s.astype(_BF16) + bias2)
                    acc_sc[slot] += jnp.dot(vt_ref[0, t, n], p, preferred_element_type=_F32)
            return carry

        lax.fori_loop(0, n_steps, body, 0)
        write_out()

    @pl.when(jnp.logical_not(fast))
    def _attend_two_pass():
        def max_body(t, ms):
            r0, bias2 = step_bias(t)
            out = []
            for n in range(ATT_KV_HEADS):
                k_rows = k_ref[0, pl.ds(r0, KS), n * ATT_DH:(n + 1) * ATT_DH]
                for pr in range(n_pair):
                    s = jnp.dot(k_rows, rhs[n][pr], preferred_element_type=_F32).astype(_BF16) + bias2
                    out.append(jnp.maximum(ms[n * n_pair + pr],
                                           jnp.max(s, axis=0, keepdims=True).astype(_F32)))
            return tuple(out)

        ms = lax.fori_loop(0, n_steps, max_body,
                           tuple(jnp.full((1, HPS * nq), NEG_BIG, _F32) for _ in range(n_slot)))

        first_row = lax.broadcasted_iota(jnp.int32, (ATT_DH, 1), 0) == 0
        ones_col = jnp.where(lax.broadcasted_iota(jnp.int32, (KS, ATT_DH), 1) == 0, 1.0, 0.0).astype(_BF16)
        rhs_aug = []
        for n in range(ATT_KV_HEADS):
            for pr in range(n_pair):
                m_ref = jnp.where(ms[n * n_pair + pr] > 0.5 * NEG_BIG, ms[n * n_pair + pr], 0.0)
                shift = jnp.where(first_row, -m_ref, 0.0).astype(_BF16)
                rhs_aug.append(jnp.concatenate([rhs[n][pr], shift], axis=0))
        acc_sc[...] = jnp.zeros_like(acc_sc)

        def pv_body(t, carry):
            r0, bias2 = step_bias(t)
            for n in range(ATT_KV_HEADS):
                k_aug = jnp.concatenate([k_ref[0, pl.ds(r0, KS), n * ATT_DH:(n + 1) * ATT_DH], ones_col], axis=1)
                for pr in range(n_pair):
                    slot = n * n_pair + pr
                    s = jnp.dot(k_aug, rhs_aug[slot], preferred_element_type=_F32)
                    p = jnp.exp2(s.astype(_BF16) + bias2)
                    acc_sc[slot] += jnp.dot(vt_ref[0, t, n], p, preferred_element_type=_F32)
            return carry

        lax.fori_loop(0, n_steps, pv_body, 0)
        write_out()


def _dsa_attn(q, qi, wi, k, v, ki, logit_bound, *, causal, lk_true, topk, n_valid_q, qb_size=128):
    bsz, lq, _ = q.shape
    lk = k.shape[1]
    ns = lk // KS
    assert lq % qb_size == 0 and lk % KS == 0 and KS % CS == 0 and KS % NCLS == 0 and topk <= NCLS
    wi_t = jnp.swapaxes(wi, 1, 2)
    vt = v.astype(_BF16).reshape(bsz, ns, KS, ATT_KV_HEADS, ATT_DH).transpose(0, 1, 3, 4, 2)
    ones_rows = jnp.zeros((bsz, ns, ATT_KV_HEADS, 8, KS), _BF16).at[:, :, :, 0, :].set(1.0)
    vt = jnp.concatenate([vt, ones_rows], axis=3)

    def whole(shape):
        nd = len(shape)
        return pl.BlockSpec((1,) + shape, lambda b, i: (b,) + (0,) * nd, pipeline_mode=pl.Buffered(1))

    def qrows(wd):
        return pl.BlockSpec((1, qb_size, wd), lambda b, i: (b, i, 0))

    return pl.pallas_call(
        functools.partial(_dsa_attn_kernel, qb_size=qb_size, topk=topk, causal=causal,
                          lk_true=lk_true, n_valid_q=n_valid_q),
        grid=(bsz, lq // qb_size),
        in_specs=[pl.BlockSpec(memory_space=pltpu.SMEM), qrows(_QW), qrows(_QIW),
                  pl.BlockSpec((1, IDX_HEADS, qb_size), lambda b, i: (b, 0, i)),
                  whole((lk, _KVW)), whole((ns, ATT_KV_HEADS, ATT_DH + 8, KS)), whole((lk, IDX_DIM))],
        out_specs=qrows(_QW),
        out_shape=jax.ShapeDtypeStruct((bsz, lq, _QW), _BF16),
        scratch_shapes=[pltpu.VMEM((lk, qb_size), jnp.int32),
                        pltpu.VMEM((NCLS, qb_size), jnp.int32),
                        pltpu.VMEM((ATT_KV_HEADS * ATT_GROUP // HPS, ATT_DH + 8, HPS * qb_size), _F32)],
        compiler_params=_params("arbitrary", "arbitrary"),
    )(jnp.reshape(logit_bound, (1,)).astype(_F32), q, qi, wi_t, k.astype(_BF16), vt, ki.astype(_BF16))


def _hgrn_lower_bounds(logits):
    p = jax.nn.softmax(logits.astype(_F32), axis=0)
    return jnp.maximum(jnp.cumsum(p, axis=0) - p[:1], 0.0)


def _pad_rows(x, n):
    return jnp.pad(x, ((0, 0), (0, n - x.shape[1]), (0, 0)))


def kernel(x_prompt, x_sample, state_hgrn, cache_k, cache_v, cache_kidx, norm_ffn1, ffn1_w_in, ffn1_w_out, norm_mix, norm_ffn2, ffn2_w_in, ffn2_w_out, hgrn_w_in, hgrn_lb_logits, hgrn_onorm, hgrn_w_out, dsa_w_in, dsa_qnorm, dsa_knorm, dsa_w_out):
    bp, lp, d = x_prompt.shape
    bs, ls, _ = x_sample.shape
    past = cache_k.shape[2]
    depth = norm_ffn1.shape[0]
    assert ls == SUB and lp % 256 == 0
    lb_all = _hgrn_lower_bounds(hgrn_lb_logits)
    tab_p = _rope_tables(jnp.arange(lp, dtype=jnp.int32))
    tab_s = _rope_tables(past + jnp.arange(ls, dtype=jnp.int32))
    lk_s = past + ls
    qb = 128
    lk_pad = -(-lk_s // KS) * KS

    xp = x_prompt.reshape(bp * lp, d)
    xs = x_sample.reshape(bs * ls, d)
    hg_p, hg_s, kp, vp, kip, ksm, vsm, kism = [], [], [], [], [], [], [], []
    for i in range(depth):
        w1 = _ffn_weights(ffn1_w_in[i], ffn1_w_out[i])
        xp = _ffn(xp, norm_ffn1[i], *w1)
        xs = _ffn(xs, norm_ffn1[i], *w1)
        j = i // 2
        if i % 2 == 0:
            w_in = hgrn_w_in[j].astype(_BF16)
            w_out = hgrn_w_out[j].astype(_BF16)
            hp = _linear(xp, w_in, g=norm_mix[i]).reshape(bp, lp, -1)
            hs = _linear(xs, w_in, g=norm_mix[i]).reshape(bs, ls, -1)
            s0 = jnp.zeros((bp,) + state_hgrn.shape[2:], _F32)
            yp, sp = _hgrn_core(hp, s0, lb_all[j], hgrn_onorm[j], tb=256)
            ys, ss = _hgrn_core(hs, state_hgrn[j], lb_all[j], hgrn_onorm[j], tb=ls)
            hg_p.append(sp)
            hg_s.append(ss)
            xp = _linear(yp.reshape(bp * lp, d), w_out, res=xp)
            xs = _linear(ys.reshape(bs * ls, d), w_out, res=xs)
        else:
            w_in = _dsa_in_weight(dsa_w_in[j])
            w_out = dsa_w_out[j].astype(_BF16)
            hp = _linear(xp, w_in, g=norm_mix[i]).reshape(bp, lp, -1)
            hs = _linear(xs, w_in, g=norm_mix[i]).reshape(bs, ls, -1)
            q1, k1, v1, qi1, ki1, wi1 = _dsa_post(hp, dsa_qnorm[j], dsa_knorm[j], tab_p, tb=256)
            q2, k2, v2, qi2, ki2, wi2 = _dsa_post(hs, dsa_qnorm[j], dsa_knorm[j], tab_s, tb=ls)
            bound = (1.01 * ATT_DH ** 0.5 * LOG2E) * jnp.max(jnp.abs(dsa_qnorm[j])) * jnp.max(jnp.abs(dsa_knorm[j]))
            op = _dsa_attn(q1, qi1, wi1, k1, v1, ki1, bound, causal=True, lk_true=lp,
                           topk=min(TOPK_MAX, lp // 4), n_valid_q=qb, qb_size=qb)
            k_all = _pad_rows(jnp.concatenate([cache_k[j].reshape(bs, past, _KVW), k2], axis=1), lk_pad)
            v_all = _pad_rows(jnp.concatenate([cache_v[j].reshape(bs, past, _KVW), v2], axis=1), lk_pad)
            ki_all = _pad_rows(jnp.concatenate([cache_kidx[j], ki2], axis=1), lk_pad)
            os_ = _dsa_attn(_pad_rows(q2, qb), _pad_rows(qi2, qb), _pad_rows(wi2, qb), k_all, v_all, ki_all,
                            jnp.inf, causal=False, lk_true=lk_s, topk=min(TOPK_MAX, lk_s // 4), n_valid_q=ls,
                            qb_size=qb)[:, :ls]
            kp.append(k1.reshape(bp, lp, ATT_KV_HEADS, ATT_DH))
            vp.append(v1.reshape(bp, lp, ATT_KV_HEADS, ATT_DH))
            kip.append(ki1)
            ksm.append(k2.reshape(bs, ls, ATT_KV_HEADS, ATT_DH))
            vsm.append(v2.reshape(bs, ls, ATT_KV_HEADS, ATT_DH))
            kism.append(ki2)
            xp = _linear(op.reshape(bp * lp, d), w_out, res=xp)
            xs = _linear(os_.reshape(bs * ls, d), w_out, res=xs)
        w2 = _ffn_weights(ffn2_w_in[i], ffn2_w_out[i])
        xp = _ffn(xp, norm_ffn2[i], *w2)
        xs = _ffn(xs, norm_ffn2[i], *w2)
    return (xp.reshape(bp, lp, d), xs.reshape(bs, ls, d), jnp.stack(hg_p),
            jnp.stack(kp), jnp.stack(vp), jnp.stack(kip), jnp.stack(hg_s),
            jnp.stack(ksm), jnp.stack(vsm), jnp.stack(kism))
```

```python
import functools

import numpy as np
import jax
import jax.numpy as jnp
from jax import lax
from jax.experimental import pallas as pl
from jax.experimental.pallas import tpu as pltpu

EPS = 1e-6
CHUNK = 64
HG_HEADS = 8
HG_DK = 128
ATT_HEADS = 8
ATT_KV_HEADS = 2
ATT_GROUP = ATT_HEADS // ATT_KV_HEADS
ATT_DH = 128
IDX_HEADS = 8
IDX_DIM = 64
TOPK_MAX = 256
ROPE_THETA = 10000.0

LANES = 128
SUB = 16
HALF = SUB // 2
KS = 1024
CS = 512
HPS = 4
NCLS = 256
SEARCH_UNROLL = 2
FUSED_COUNTS = 24
LOG2E = 1.4426950408889634
VMEM_LIMIT = 56 * 1024 * 1024

INT_MIN = -2 ** 31
KEY_NEG_INF = -2 ** 31 + 0x7FFFFF
NEG_BIG = -1e30
LOGIT_SAFE = 100.0
LOGF_FLOOR = 1e-37

_F32 = jnp.float32
_BF16 = jnp.bfloat16


def _params(*sem):
    return pltpu.CompilerParams(dimension_semantics=sem, vmem_limit_bytes=VMEM_LIMIT)


def _resident(shape):
    nd = len(shape)
    return pl.BlockSpec(shape, lambda *_: (0,) * nd, pipeline_mode=pl.Buffered(1))


def _row_tile(t, cap=512):
    for tm in (1024, 512, 256, 128, 64, 32, 16, 8):
        if t % tm == 0 and tm <= cap:
            return tm
    raise ValueError(f"unsupported row count {t}")


def _ffn_kernel(x_ref, g_ref, wa_ref, wb_ref, wo_ref, o_ref, acc_ref, *, n_chunks):
    x = x_ref[...]
    ms = jnp.mean(x * x, axis=-1, keepdims=True)
    xn = (x * lax.rsqrt(ms + EPS) * g_ref[...]).astype(_BF16)

    def chunk(j):
        a = jnp.dot(xn, wa_ref[j], preferred_element_type=_F32)
        b = jnp.dot(xn, wb_ref[j], preferred_element_type=_F32)
        h = (a * jax.nn.sigmoid(a) * b).astype(_BF16)
        return jnp.dot(h, wo_ref[j], preferred_element_type=_F32)

    acc_ref[...] = chunk(0)

    def body(j, carry):
        acc_ref[...] += chunk(j)
        return carry

    lax.fori_loop(1, n_chunks, body, 0)
    o_ref[...] = x_ref[...] + 0.5 * acc_ref[...]


def _ffn_weights(w_in, w_out, tf=256):
    d, two_ff = w_in.shape
    d_ff = two_ff // 2
    n = d_ff // tf
    w = w_in.astype(_BF16).reshape(d, 2, n, tf).transpose(1, 2, 0, 3)
    return w[0], w[1], w_out.astype(_BF16).reshape(n, tf, d)


def _ffn(x, g, wa, wb, wo):
    t, d = x.shape
    n, _, tf = wa.shape
    tm = _row_tile(t, cap=1024)
    return pl.pallas_call(
        functools.partial(_ffn_kernel, n_chunks=n),
        grid=(t // tm,),
        in_specs=[pl.BlockSpec((tm, d), lambda i: (i, 0)),
                  _resident((1, d)), _resident((n, d, tf)), _resident((n, d, tf)),
                  _resident((n, tf, d))],
        out_specs=pl.BlockSpec((tm, d), lambda i: (i, 0)),
        out_shape=jax.ShapeDtypeStruct((t, d), _F32),
        scratch_shapes=[pltpu.VMEM((tm, d), _F32)],
        compiler_params=_params("arbitrary"),
    )(x, g.reshape(1, d), wa, wb, wo)


def _linear_kernel(*refs, has_norm, has_res):
    refs = list(refs)
    x_ref = refs.pop(0)
    g_ref = refs.pop(0) if has_norm else None
    w_ref = refs.pop(0)
    r_ref = refs.pop(0) if has_res else None
    o_ref = refs.pop(0)
    x = x_ref[...]
    if has_norm:
        ms = jnp.mean(x * x, axis=-1, keepdims=True)
        x = x * lax.rsqrt(ms + EPS) * g_ref[...]
    y = jnp.dot(x.astype(_BF16), w_ref[...], preferred_element_type=_F32)
    if has_res:
        y = y + r_ref[...]
    o_ref[...] = y


def _linear(x, w, g=None, res=None):
    t, k = x.shape
    n = w.shape[1]
    tm = _row_tile(t)
    tn = n if n <= 2304 else 1024
    in_specs = [pl.BlockSpec((tm, k), lambda i, j: (i, 0))]
    args = [x]
    if g is not None:
        in_specs.append(_resident((1, k)))
        args.append(g.reshape(1, k))
    if tn == n:
        in_specs.append(_resident((k, n)))
    else:
        in_specs.append(pl.BlockSpec((k, tn), lambda i, j: (0, j)))
    args.append(w)
    if res is not None:
        in_specs.append(pl.BlockSpec((tm, tn), lambda i, j: (i, j)))
        args.append(res)
    return pl.pallas_call(
        functools.partial(_linear_kernel, has_norm=g is not None, has_res=res is not None),
        grid=(t // tm, n // tn),
        in_specs=in_specs,
        out_specs=pl.BlockSpec((tm, tn), lambda i, j: (i, j)),
        out_shape=jax.ShapeDtypeStruct((t, n), _F32),
        compiler_params=_params("arbitrary", "arbitrary"),
    )(*args)


def _hgrn_kernel(q_ref, f_ref, v_ref, g_ref, s0_ref, lb_ref, on_ref, ltri_ref, utri_ref,
                 y_ref, sfin_ref,
                 st_sc, qs_sc, ks_sc, qd_sc, kd_sc, b_sc, o_sc, *, n_sub):
    t = pl.program_id(1)

    @pl.when(t == 0)
    def _init():
        for h in range(HG_HEADS):
            st_sc[h] = s0_ref[0, h].T

    lb = lb_ref[...]
    q = q_ref[0]
    fg = lb + (1.0 - lb) * jax.nn.sigmoid(f_ref[0])
    logf = jnp.log(jnp.maximum(fg, LOGF_FLOOR))
    kk = 1.0 - fg
    qs = q * jax.nn.sigmoid(q)
    logf_hi = logf.astype(_BF16)
    logf_lo = (logf - logf_hi.astype(_F32)).astype(_BF16)

    def segsum(tri_ref):
        return (jnp.dot(tri_ref[...], logf_hi, preferred_element_type=_F32)
                + jnp.dot(tri_ref[...], logf_lo, preferred_element_type=_F32))

    bcum = segsum(ltri_ref)
    brem = segsum(utri_ref)
    qs_sc[...] = qs
    ks_sc[...] = kk
    b_sc[...] = bcum * LOG2E
    qd_sc[...] = qs * jnp.exp(bcum)
    kd_sc[...] = kk * jnp.exp(brem)

    ones = jnp.ones((HG_DK, LANES), _BF16)
    tio = lax.broadcasted_iota(jnp.int32, (SUB, 1), 0)

    def sub_body(j, carry):
        r0 = pl.multiple_of(j * SUB, SUB)
        rows = pl.ds(r0, SUB)
        for h in range(HG_HEADS):
            cs = slice(h * HG_DK, (h + 1) * HG_DK)
            st = st_sc[h]
            qd = qd_sc[rows, cs]
            acc = lax.dot_general(qd.astype(_BF16), st.astype(_BF16),
                                  (((1,), (1,)), ((), ())), preferred_element_type=_F32)
            bq = b_sc[rows, cs]
            qr = qs_sc[rows, cs]
            kr = ks_sc[rows, cs]
            vv = v_ref[0, rows, cs]
            terms = []
            for s in range(SUB):
                t0 = 0 if s < HALF else HALF
                dm = jnp.where(tio[t0:] >= s, bq[t0:] - bq[s:s + 1, :], -jnp.inf)
                terms.append(qr[t0:] * kr[s:s + 1, :] * jnp.exp2(dm))
            tcat = jnp.concatenate(terms, axis=0).astype(_BF16)
            rsum = jnp.dot(tcat, ones, preferred_element_type=_F32)
            top, bot, off = acc[:HALF], acc[HALF:], 0
            for s in range(SUB):
                if s < HALF:
                    top = top + rsum[off:off + HALF, :] * vv[s:s + 1, :]
                    off += HALF
                bot = bot + rsum[off:off + HALF, :] * vv[s:s + 1, :]
                off += HALF
            o_sc[rows, cs] = jnp.concatenate([top, bot], axis=0)
            kd = kd_sc[rows, cs]
            upd = jnp.dot(vv.T.astype(_BF16), kd.astype(_BF16), preferred_element_type=_F32)
            st_sc[h] = jnp.exp2(bq[SUB - 1:SUB, :]) * st + upd
        return carry

    lax.fori_loop(0, n_sub, sub_body, 0)

    g = g_ref[0]
    gate = g * jax.nn.sigmoid(g)
    for h in range(HG_HEADS):
        cs = slice(h * HG_DK, (h + 1) * HG_DK)
        oh = o_sc[:, cs]
        ms = jnp.mean(oh * oh, axis=-1, keepdims=True)
        y_ref[0, :, cs] = oh * lax.rsqrt(ms + EPS) * on_ref[...] * gate[:, cs]

    @pl.when(t == pl.num_programs(1) - 1)
    def _fin():
        for h in range(HG_HEADS):
            sfin_ref[0, h] = st_sc[h].T


def _hgrn_core(hproj, s0, lb, onorm, tb):
    bsz, l, w4 = hproj.shape
    w = w4 // 4
    dv = w // HG_HEADS
    assert dv == LANES and HG_DK == LANES and tb % SUB == 0 and l % tb == 0
    idx = np.arange(tb)
    same = (idx[:, None] // SUB) == (idx[None, :] // SUB)
    ltri = jnp.asarray((same & (idx[None, :] <= idx[:, None])).astype(np.float32)).astype(_BF16)
    utri = jnp.asarray((same & (idx[None, :] > idx[:, None])).astype(np.float32)).astype(_BF16)

    def col(c):
        return pl.BlockSpec((1, tb, w), lambda b, t: (b, t, c))

    return pl.pallas_call(
        functools.partial(_hgrn_kernel, n_sub=tb // SUB),
        grid=(bsz, l // tb),
        in_specs=[col(0), col(1), col(2), col(3),
                  pl.BlockSpec((1, HG_HEADS, HG_DK, dv), lambda b, t: (b, 0, 0, 0)),
                  _resident((1, w)), _resident((1, dv)), _resident((tb, tb)), _resident((tb, tb))],
        out_specs=[pl.BlockSpec((1, tb, w), lambda b, t: (b, t, 0)),
                   pl.BlockSpec((1, HG_HEADS, HG_DK, dv), lambda b, t: (b, 0, 0, 0))],
        out_shape=[jax.ShapeDtypeStruct((bsz, l, w), _F32),
                   jax.ShapeDtypeStruct((bsz, HG_HEADS, HG_DK, dv), _F32)],
        scratch_shapes=[pltpu.VMEM((HG_HEADS, dv, HG_DK), _F32)] + [pltpu.VMEM((tb, w), _F32)] * 6,
        compiler_params=_params("arbitrary", "arbitrary"),
    )(hproj, hproj, hproj, hproj, s0, lb.reshape(1, w), onorm.reshape(1, dv), ltri, utri)


_QW = ATT_HEADS * ATT_DH
_KVW = ATT_KV_HEADS * ATT_DH
_QIW = IDX_HEADS * IDX_DIM
_KI_OFF = _QW + 2 * _KVW + _QIW
_WI_OFF = _KI_OFF + LANES
_DSA_PAD = _WI_OFF + LANES


def _dsa_post_kernel(h_ref, qn_ref, kn_ref, c1_ref, s1_ref, c2_ref, s2_ref,
                     q_ref, k_ref, v_ref, qi_ref, ki_ref, wi_ref):
    c1, s1, c2, s2 = c1_ref[...], s1_ref[...], c2_ref[...], s2_ref[...]
    first_half = (lax.broadcasted_iota(jnp.int32, (1, LANES), 1) % IDX_DIM) < (IDX_DIM // 2)

    def norm_rope(x, gain):
        ms = jnp.mean(x * x, axis=-1, keepdims=True)
        x = x * lax.rsqrt(ms + EPS) * gain
        return x * c1 + pltpu.roll(x, ATT_DH // 2, 1) * s1

    def rope_idx(x):
        swapped = jnp.where(first_half, pltpu.roll(x, LANES - IDX_DIM // 2, 1),
                            pltpu.roll(x, IDX_DIM // 2, 1))
        return x * c2 + swapped * s2

    for h in range(ATT_HEADS):
        cs = slice(h * ATT_DH, (h + 1) * ATT_DH)
        q_ref[0, :, cs] = (norm_rope(h_ref[0, :, cs], qn_ref[...]) * (ATT_DH ** -0.5 * LOG2E)).astype(_BF16)
    for n in range(ATT_KV_HEADS):
        cs = slice(n * ATT_DH, (n + 1) * ATT_DH)
        k_ref[0, :, cs] = norm_rope(h_ref[0, :, _QW + n * ATT_DH:_QW + (n + 1) * ATT_DH], kn_ref[...])
    v_ref[0] = h_ref[0, :, _QW + _KVW:_QW + 2 * _KVW]
    for p in range(_QIW // LANES):
        cs = slice(p * LANES, (p + 1) * LANES)
        qi_ref[0, :, cs] = rope_idx(h_ref[0, :, _QW + 2 * _KVW + p * LANES:_QW + 2 * _KVW + (p + 1) * LANES]).astype(_BF16)
    ki_ref[0] = rope_idx(h_ref[0, :, _KI_OFF:_KI_OFF + LANES])[:, :IDX_DIM]
    wi_ref[0] = h_ref[0, :, _WI_OFF:_WI_OFF + IDX_HEADS] * ((IDX_HEADS ** -0.5) * (IDX_DIM ** -0.5))


def _rope_tables(pos):
    def table(d, reps):
        half = d // 2
        inv_freq = jnp.power(ROPE_THETA, -jnp.arange(half, dtype=_F32) * (2.0 / d))
        ang = pos.astype(_F32)[:, None] * inv_freq[None, :]
        c, s = jnp.cos(ang), jnp.sin(ang)
        return (jnp.tile(jnp.concatenate([c, c], axis=-1), (1, reps)),
                jnp.tile(jnp.concatenate([-s, s], axis=-1), (1, reps)))
    c1, s1 = table(ATT_DH, LANES // ATT_DH)
    c2, s2 = table(IDX_DIM, LANES // IDX_DIM)
    return c1, s1, c2, s2


def _dsa_post(hproj, qn, kn, tables, tb):
    bsz, l, _ = hproj.shape

    def rows(wd):
        return pl.BlockSpec((1, tb, wd), lambda b, t: (b, t, 0))

    tab = pl.BlockSpec((tb, LANES), lambda b, t: (t, 0))
    return pl.pallas_call(
        _dsa_post_kernel,
        grid=(bsz, l // tb),
        in_specs=[rows(_DSA_PAD), _resident((1, ATT_DH)), _resident((1, ATT_DH)), tab, tab, tab, tab],
        out_specs=[rows(_QW), rows(_KVW), rows(_KVW), rows(_QIW), rows(IDX_DIM), rows(IDX_HEADS)],
        out_shape=[jax.ShapeDtypeStruct((bsz, l, _QW), _BF16),
                   jax.ShapeDtypeStruct((bsz, l, _KVW), _F32),
                   jax.ShapeDtypeStruct((bsz, l, _KVW), _F32),
                   jax.ShapeDtypeStruct((bsz, l, _QIW), _BF16),
                   jax.ShapeDtypeStruct((bsz, l, IDX_DIM), _F32),
                   jax.ShapeDtypeStruct((bsz, l, IDX_HEADS), _F32)],
        compiler_params=_params("arbitrary", "arbitrary"),
    )(hproj, qn.reshape(1, ATT_DH), kn.reshape(1, ATT_DH), *tables)


def _dsa_in_weight(w_in):
    d = w_in.shape[0]
    z = lambda n: jnp.zeros((d, n), w_in.dtype)
    ki0 = _QW + 2 * _KVW + _QIW
    return jnp.concatenate([w_in[:, :ki0], w_in[:, ki0:ki0 + IDX_DIM], z(LANES - IDX_DIM),
                            w_in[:, ki0 + IDX_DIM:], z(LANES - IDX_HEADS)], axis=1).astype(_BF16)


def _dsa_attn_kernel(bound_ref, q_ref, qi_ref, wi_ref, k_ref, vt_ref, ki_ref, o_ref,
                     key_sc, cmax_sc, acc_sc, thr_sc, *, qb_size, topk, causal, lk_true, n_valid_q):
    i = pl.program_id(1)
    n_blk = pl.num_programs(1) - 1
    nq = qb_size
    has_new = i < n_blk
    has_old = i >= 1
    cur = i % 2
    key_cur = key_sc.at[cur]
    key_old = key_sc.at[1 - cur]
    lane_q = lax.broadcasted_iota(jnp.int32, (1, nq), 1)

    def key_steps(blk):
        if causal:
            return ((blk + 1) * nq + KS - 1) // KS
        return k_ref.shape[1] // KS

    n_s = jnp.where(has_new, key_steps(i), 0)
    n_a = jnp.where(has_old, key_steps(i - 1), 0)
    if causal:
        limit = ((i * nq + lane_q) // CHUNK + 1) * CHUNK
    else:
        limit = jnp.full((1, nq), lk_true, jnp.int32)
    row_io = lax.broadcasted_iota(jnp.int32, (LANES, 1), 0)

    @pl.when(has_new)
    def _scores():
        qi_t = qi_ref[0].astype(_F32).T.astype(_BF16)
        qi_pairs = [jnp.concatenate([qi_t[(2 * p + e) * IDX_DIM:(2 * p + e + 1) * IDX_DIM, :]
                                     for e in range(2)], axis=1) for p in range(IDX_HEADS // 2)]
        wi = wi_ref[0]
        cmax_sc[...] = jnp.full(cmax_sc.shape, INT_MIN, jnp.int32)

        def score_body(t, carry):
            for u in range(KS // LANES):
                r0 = pl.multiple_of(t * KS + u * LANES, LANES)
                kit = ki_ref[0, pl.ds(r0, LANES), :]
                acc = jnp.zeros((LANES, nq), _F32)
                for p in range(IDX_HEADS // 2):
                    d = jnp.dot(kit, qi_pairs[p], preferred_element_type=_F32)
                    for e in range(2):
                        h = 2 * p + e
                        acc = acc + jnp.maximum(d[:, e * nq:(e + 1) * nq], 0.0) * wi[h:h + 1, :]
                acc = jnp.where(r0 + row_io < limit, acc, -jnp.inf)
                bits = pltpu.bitcast(acc, jnp.int32)
                key = bits ^ ((bits >> 31) & 0x7FFFFFFF)
                key_cur[pl.ds(r0, LANES), :] = key
                cls = slice((u % (NCLS // LANES)) * LANES, (u % (NCLS // LANES) + 1) * LANES)
                cmax_sc[cls, :] = jnp.maximum(cmax_sc[cls, :], key)
            return carry

        lax.fori_loop(0, n_s, score_body, 0)

    n_cs = n_s * (KS // CS)

    def count_chunk(c, cand, piece):
        r0 = pl.multiple_of(piece * CS, CS)
        m = (key_cur[pl.ds(r0, CS), :] >= cand).astype(jnp.int32)
        return c + jnp.sum(m.reshape(CS // 8, 8, nq), axis=0)

    def count_from(cand, first, c):
        c = lax.fori_loop(first, n_cs, lambda p, c: count_chunk(c, cand, p), c)
        return jnp.sum(c, axis=0, keepdims=True)

    def active(lo, hi, cnt_lo):
        return jnp.where((cnt_lo != topk) & (hi - 1 > lo), 1.0, 0.0)

    def midpoint(lo, hi):
        return (lo >> 1) + (hi >> 1) + (lo & hi & 1)

    def narrow(st, cand, c, gate):
        lo, hi, cnt_lo, cnt_hi = st
        act = (active(lo, hi, cnt_lo) > 0) & gate
        up = act & (c >= topk)
        dn = act & (c < topk)
        return (jnp.where(up, cand, lo), jnp.where(dn, cand, hi),
                jnp.where(up, c, cnt_lo), jnp.where(dn, c, cnt_hi))

    zero8 = jnp.zeros((8, nq), jnp.int32)
    always = lane_q >= 0
    cm = cmax_sc[...]
    lo0 = jnp.min(cm, axis=0, keepdims=True)
    top = jnp.max(cm, axis=0, keepdims=True)
    hi0 = jnp.where(top == 2 ** 31 - 1, top, top + 1)
    state = (lo0, hi0, count_from(lo0, 0, zero8), jnp.zeros((1, nq), jnp.int32))

    thr_old = thr_sc[1 - cur][0:1, :]
    q_t = q_ref[0].astype(_F32).T.astype(_BF16)
    n_pair = ATT_GROUP // HPS
    rhs = [[jnp.concatenate([q_t[(n * ATT_GROUP + HPS * pr + e) * ATT_DH:(n * ATT_GROUP + HPS * pr + e + 1) * ATT_DH, :]
                             for e in range(HPS)], axis=1) for pr in range(n_pair)]
           for n in range(ATT_KV_HEADS)]
    n_slot = ATT_KV_HEADS * n_pair

    def step_bias(t):
        r0 = pl.multiple_of(t * KS, KS)
        bias = jnp.where(key_old[pl.ds(r0, KS), :] >= thr_old, 0.0, NEG_BIG).astype(_BF16)
        return r0, jnp.concatenate([bias] * HPS, axis=1)

    def attend_unshifted(t, between=None):
        r0, bias2 = step_bias(t)
        for n in range(ATT_KV_HEADS):
            if between is not None:
                between(n)
            k_rows = k_ref[0, pl.ds(r0, KS), n * ATT_DH:(n + 1) * ATT_DH]
            for pr in range(n_pair):
                slot = n * n_pair + pr
                s = jnp.dot(k_rows, rhs[n][pr], preferred_element_type=_F32)
                p = jnp.exp2(s.astype(_BF16) + bias2)
                if between is not None:
                    between(n)
                acc_sc[slot] += jnp.dot(vt_ref[0, t, n], p, preferred_element_type=_F32)

    fast = bound_ref[0] <= LOGIT_SAFE
    acc_sc[...] = jnp.zeros_like(acc_sc)

    last_piece = jnp.maximum(n_cs - 1, 0)

    def fused_body(t, carry):
        box = [carry[:4], carry[4], carry[5], carry[6]]

        def counts(_n):
            st, cand, c8, piece = box
            for _ in range(FUSED_COUNTS // (2 * ATT_KV_HEADS)):
                c8 = count_chunk(c8, cand, jnp.minimum(piece, last_piece))
                piece = piece + 1
                done = piece >= n_cs
                gate = (jnp.zeros((1, nq), jnp.int32) + done.astype(jnp.int32)) > 0
                st = narrow(st, cand, jnp.sum(c8, axis=0, keepdims=True), gate)
                cand = jnp.where(gate, midpoint(st[0], st[1]), cand)
                c8 = jnp.where(jnp.zeros((8, nq), jnp.int32) + done.astype(jnp.int32) > 0, 0, c8)
                piece = jnp.where(done, 0, piece)
            box[:] = [st, cand, c8, piece]

        attend_unshifted(t, counts)
        st, cand, c8, piece = box
        return st + (cand, c8, piece)

    carry = lax.fori_loop(0, jnp.where(fast, n_a, 0), fused_body,
                          state + (midpoint(lo0, hi0), zero8, jnp.int32(0)))
    state, cand, c8, piece = carry[:4], carry[4], carry[5], carry[6]

    @pl.when(jnp.logical_not(fast) & has_old)
    def _attend_two_pass():
        def max_body(t, ms):
            r0, bias2 = step_bias(t)
            out = []
            for n in range(ATT_KV_HEADS):
                k_rows = k_ref[0, pl.ds(r0, KS), n * ATT_DH:(n + 1) * ATT_DH]
                for pr in range(n_pair):
                    s = jnp.dot(k_rows, rhs[n][pr], preferred_element_type=_F32).astype(_BF16) + bias2
                    out.append(jnp.maximum(ms[n * n_pair + pr],
                                           jnp.max(s, axis=0, keepdims=True).astype(_F32)))
            return tuple(out)

        ms = lax.fori_loop(0, n_a, max_body,
                           tuple(jnp.full((1, HPS * nq), NEG_BIG, _F32) for _ in range(n_slot)))

        first_row = lax.broadcasted_iota(jnp.int32, (ATT_DH, 1), 0) == 0
        ones_col = jnp.where(lax.broadcasted_iota(jnp.int32, (KS, ATT_DH), 1) == 0, 1.0, 0.0).astype(_BF16)
        rhs_aug = []
        for n in range(ATT_KV_HEADS):
            for pr in range(n_pair):
                m_ref = jnp.where(ms[n * n_pair + pr] > 0.5 * NEG_BIG, ms[n * n_pair + pr], 0.0)
                shift = jnp.where(first_row, -m_ref, 0.0).astype(_BF16)
                rhs_aug.append(jnp.concatenate([rhs[n][pr], shift], axis=0))

        def pv_body(t, carry):
            r0, bias2 = step_bias(t)
            for n in range(ATT_KV_HEADS):
                k_aug = jnp.concatenate([k_ref[0, pl.ds(r0, KS), n * ATT_DH:(n + 1) * ATT_DH], ones_col], axis=1)
                for pr in range(n_pair):
                    slot = n * n_pair + pr
                    s = jnp.dot(k_aug, rhs_aug[slot], preferred_element_type=_F32)
                    p = jnp.exp2(s.astype(_BF16) + bias2)
                    acc_sc[slot] += jnp.dot(vt_ref[0, t, n], p, preferred_element_type=_F32)
            return carry

        lax.fori_loop(0, n_a, pv_body, 0)

    @pl.when(has_old)
    def _write_out():
        for n in range(ATT_KV_HEADS):
            for pr in range(n_pair):
                a = acc_sc[n * n_pair + pr]
                o_t = a[:ATT_DH, :] / a[ATT_DH:ATT_DH + 1, :]
                for e in range(HPS):
                    hh = n * ATT_GROUP + HPS * pr + e
                    o_ref[0, :, hh * ATT_DH:(hh + 1) * ATT_DH] = o_t[:, e * nq:(e + 1) * nq].T.astype(_BF16)

    state = narrow(state, cand, count_from(cand, piece, c8), always)

    def halve(st):
        cand = midpoint(st[0], st[1])
        return narrow(st, cand, count_from(cand, 0, zero8), always)

    def search_body(st):
        st = st[:4]
        for _ in range(SEARCH_UNROLL):
            st = halve(st)
        return st + (jnp.max(active(st[0], st[1], st[2])),)

    thr, _, n_ge, n_gt, _ = lax.while_loop(
        lambda st: st[4] > 0, search_body,
        state + (jnp.max(active(state[0], state[1], state[2])),))

    need = (topk - n_gt).astype(_F32)
    over = jnp.max(jnp.where(lane_q < n_valid_q, n_ge, 0)) > topk

    @pl.when(over & has_new)
    def _drop_ties():
        rr = lax.broadcasted_iota(jnp.int32, (KS, KS), 0)
        cc = lax.broadcasted_iota(jnp.int32, (KS, KS), 1)
        strict = jnp.where(cc < rr, 1.0, 0.0).astype(_BF16)

        def body(t, run):
            r0 = pl.multiple_of(t * KS, KS)
            kk = key_cur[pl.ds(r0, KS), :]
            tie = kk == thr
            tf = jnp.where(tie, 1.0, 0.0)
            before = jnp.dot(strict, tf.astype(_BF16), preferred_element_type=_F32) + run
            key_cur[pl.ds(r0, KS), :] = jnp.where(tie & (before >= need), KEY_NEG_INF, kk)
            return run + jnp.sum(tf, axis=0, keepdims=True)

        lax.fori_loop(0, n_s, body, jnp.zeros((1, nq), _F32))

    thr_sc[cur] = jnp.broadcast_to(jnp.maximum(thr, KEY_NEG_INF + 1), (8, nq))


def _dsa_attn(q, qi, wi, k, v, ki, logit_bound, *, causal, lk_true, topk, n_valid_q, qb_size=128):
    bsz, lq, _ = q.shape
    lk = k.shape[1]
    ns = lk // KS
    n_blk = lq // qb_size
    assert lq % qb_size == 0 and lk % KS == 0 and KS % CS == 0 and KS % NCLS == 0 and topk <= NCLS
    wi_t = jnp.swapaxes(wi, 1, 2)
    vt = v.astype(_BF16).reshape(bsz, ns, KS, ATT_KV_HEADS, ATT_DH).transpose(0, 1, 3, 4, 2)
    ones_rows = jnp.zeros((bsz, ns, ATT_KV_HEADS, 8, KS), _BF16).at[:, :, :, 0, :].set(1.0)
    vt = jnp.concatenate([vt, ones_rows], axis=3)

    def whole(shape):
        nd = len(shape)
        return pl.BlockSpec((1,) + shape, lambda b, i: (b,) + (0,) * nd, pipeline_mode=pl.Buffered(1))

    def old_rows(wd):
        return pl.BlockSpec((1, qb_size, wd), lambda b, i: (b, jnp.maximum(i - 1, 0), 0))

    def new_rows(wd):
        return pl.BlockSpec((1, qb_size, wd), lambda b, i: (b, jnp.minimum(i, n_blk - 1), 0))

    return pl.pallas_call(
        functools.partial(_dsa_attn_kernel, qb_size=qb_size, topk=topk, causal=causal,
                          lk_true=lk_true, n_valid_q=n_valid_q),
        grid=(bsz, n_blk + 1),
        in_specs=[pl.BlockSpec(memory_space=pltpu.SMEM), old_rows(_QW), new_rows(_QIW),
                  pl.BlockSpec((1, IDX_HEADS, qb_size), lambda b, i: (b, 0, jnp.minimum(i, n_blk - 1))),
                  whole((lk, _KVW)), whole((ns, ATT_KV_HEADS, ATT_DH + 8, KS)), whole((lk, IDX_DIM))],
        out_specs=old_rows(_QW),
        out_shape=jax.ShapeDtypeStruct((bsz, lq, _QW), _BF16),
        scratch_shapes=[pltpu.VMEM((2, lk, qb_size), jnp.int32),
                        pltpu.VMEM((NCLS, qb_size), jnp.int32),
                        pltpu.VMEM((ATT_KV_HEADS * ATT_GROUP // HPS, ATT_DH + 8, HPS * qb_size), _F32),
                        pltpu.VMEM((2, 8, qb_size), jnp.int32)],
        compiler_params=_params("arbitrary", "arbitrary"),
    )(jnp.reshape(logit_bound, (1,)).astype(_F32), q, qi, wi_t, k.astype(_BF16), vt, ki.astype(_BF16))


def _hgrn_lower_bounds(logits):
    p = jax.nn.softmax(logits.astype(_F32), axis=0)
    return jnp.maximum(jnp.cumsum(p, axis=0) - p[:1], 0.0)


def _pad_rows(x, n):
    return jnp.pad(x, ((0, 0), (0, n - x.shape[1]), (0, 0)))


def kernel(x_prompt, x_sample, state_hgrn, cache_k, cache_v, cache_kidx, norm_ffn1, ffn1_w_in, ffn1_w_out, norm_mix, norm_ffn2, ffn2_w_in, ffn2_w_out, hgrn_w_in, hgrn_lb_logits, hgrn_onorm, hgrn_w_out, dsa_w_in, dsa_qnorm, dsa_knorm, dsa_w_out):
    bp, lp, d = x_prompt.shape
    bs, ls, _ = x_sample.shape
    past = cache_k.shape[2]
    depth = norm_ffn1.shape[0]
    assert ls == SUB and lp % 256 == 0
    lb_all = _hgrn_lower_bounds(hgrn_lb_logits)
    tab_p = _rope_tables(jnp.arange(lp, dtype=jnp.int32))
    tab_s = _rope_tables(past + jnp.arange(ls, dtype=jnp.int32))
    lk_s = past + ls
    qb = 128
    lk_pad = -(-lk_s // KS) * KS

    xp = x_prompt.reshape(bp * lp, d)
    xs = x_sample.reshape(bs * ls, d)
    hg_p, hg_s, kp, vp, kip, ksm, vsm, kism = [], [], [], [], [], [], [], []
    for i in range(depth):
        w1 = _ffn_weights(ffn1_w_in[i], ffn1_w_out[i])
        xp = _ffn(xp, norm_ffn1[i], *w1)
        xs = _ffn(xs, norm_ffn1[i], *w1)
        j = i // 2
        if i % 2 == 0:
            w_in = hgrn_w_in[j].astype(_BF16)
            w_out = hgrn_w_out[j].astype(_BF16)
            hp = _linear(xp, w_in, g=norm_mix[i]).reshape(bp, lp, -1)
            hs = _linear(xs, w_in, g=norm_mix[i]).reshape(bs, ls, -1)
            s0 = jnp.zeros((bp,) + state_hgrn.shape[2:], _F32)
            yp, sp = _hgrn_core(hp, s0, lb_all[j], hgrn_onorm[j], tb=256)
            ys, ss = _hgrn_core(hs, state_hgrn[j], lb_all[j], hgrn_onorm[j], tb=ls)
            hg_p.append(sp)
            hg_s.append(ss)
            xp = _linear(yp.reshape(bp * lp, d), w_out, res=xp)
            xs = _linear(ys.reshape(bs * ls, d), w_out, res=xs)
        else:
            w_in = _dsa_in_weight(dsa_w_in[j])
            w_out = dsa_w_out[j].astype(_BF16)
            hp = _linear(xp, w_in, g=norm_mix[i]).reshape(bp, lp, -1)
            hs = _linear(xs, w_in, g=norm_mix[i]).reshape(bs, ls, -1)
            q1, k1, v1, qi1, ki1, wi1 = _dsa_post(hp, dsa_qnorm[j], dsa_knorm[j], tab_p, tb=256)
            q2, k2, v2, qi2, ki2, wi2 = _dsa_post(hs, dsa_qnorm[j], dsa_knorm[j], tab_s, tb=ls)
            bound = (1.01 * ATT_DH ** 0.5 * LOG2E) * jnp.max(jnp.abs(dsa_qnorm[j])) * jnp.max(jnp.abs(dsa_knorm[j]))
            op = _dsa_attn(q1, qi1, wi1, k1, v1, ki1, bound, causal=True, lk_true=lp,
                           topk=min(TOPK_MAX, lp // 4), n_valid_q=qb, qb_size=qb)
            k_all = _pad_rows(jnp.concatenate([cache_k[j].reshape(bs, past, _KVW), k2], axis=1), lk_pad)
            v_all = _pad_rows(jnp.concatenate([cache_v[j].reshape(bs, past, _KVW), v2], axis=1), lk_pad)
            ki_all = _pad_rows(jnp.concatenate([cache_kidx[j], ki2], axis=1), lk_pad)
            os_ = _dsa_attn(_pad_rows(q2, qb), _pad_rows(qi2, qb), _pad_rows(wi2, qb), k_all, v_all, ki_all,
                            jnp.inf, causal=False, lk_true=lk_s, topk=min(TOPK_MAX, lk_s // 4), n_valid_q=ls,
                            qb_size=qb)[:, :ls]
            kp.append(k1.reshape(bp, lp, ATT_KV_HEADS, ATT_DH))
            vp.append(v1.reshape(bp, lp, ATT_KV_HEADS, ATT_DH))
            kip.append(ki1)
            ksm.append(k2.reshape(bs, ls, ATT_KV_HEADS, ATT_DH))
            vsm.append(v2.reshape(bs, ls, ATT_KV_HEADS, ATT_DH))
            kism.append(ki2)
            xp = _linear(op.reshape(bp * lp, d), w_out, res=xp)
            xs = _linear(os_.reshape(bs * ls, d), w_out, res=xs)
        w2 = _ffn_weights(ffn2_w_in[i], ffn2_w_out[i])
        xp = _ffn(xp, norm_ffn2[i], *w2)
        xs = _ffn(xs, norm_ffn2[i], *w2)
    return (xp.reshape(bp, lp, d), xs.reshape(bs, ls, d), jnp.stack(hg_p),
            jnp.stack(kp), jnp.stack(vp), jnp.stack(kip), jnp.stack(hg_s),
            jnp.stack(ksm), jnp.stack(vsm), jnp.stack(kism))
```

```python
import functools

import numpy as np
import jax
import jax.numpy as jnp
from jax import lax
from jax.experimental import pallas as pl
from jax.experimental.pallas import tpu as pltpu

EPS = 1e-6
CHUNK = 64
HG_HEADS = 8
HG_DK = 128
ATT_HEADS = 8
ATT_KV_HEADS = 2
ATT_GROUP = ATT_HEADS // ATT_KV_HEADS
ATT_DH = 128
IDX_HEADS = 8
IDX_DIM = 64
TOPK_MAX = 256
ROPE_THETA = 10000.0

LANES = 128
SUB = 16
HALF = SUB // 2
KS = 1024
CS = 512
HPS = 4
NCLS = 256
SEARCH_UNROLL = 2
FUSED_COUNTS = 24
LOG2E = 1.4426950408889634
VMEM_LIMIT = 56 * 1024 * 1024

INT_MIN = -2 ** 31
KEY_NEG_INF = -2 ** 31 + 0x7FFFFF
NEG_BIG = -1e30
LOGIT_SAFE = 100.0
LOGF_FLOOR = 1e-37

_F32 = jnp.float32
_BF16 = jnp.bfloat16


def _params(*sem):
    return pltpu.CompilerParams(dimension_semantics=sem, vmem_limit_bytes=VMEM_LIMIT)


def _resident(shape):
    nd = len(shape)
    return pl.BlockSpec(shape, lambda *_: (0,) * nd, pipeline_mode=pl.Buffered(1))


def _row_tile(t, cap=512):
    for tm in (1024, 512, 256, 128, 64, 32, 16, 8):
        if t % tm == 0 and tm <= cap:
            return tm
    raise ValueError(f"unsupported row count {t}")


def _ffn_kernel(x_ref, g_ref, wa_ref, wb_ref, wo_ref, o_ref, acc_ref, *, n_chunks):
    x = x_ref[...]
    ms = jnp.mean(x * x, axis=-1, keepdims=True)
    xn = (x * lax.rsqrt(ms + EPS) * g_ref[...]).astype(_BF16)

    def chunk(j):
        a = jnp.dot(xn, wa_ref[j], preferred_element_type=_F32)
        b = jnp.dot(xn, wb_ref[j], preferred_element_type=_F32)
        h = (a * jax.nn.sigmoid(a) * b).astype(_BF16)
        return jnp.dot(h, wo_ref[j], preferred_element_type=_F32)

    acc_ref[...] = chunk(0)

    def body(j, carry):
        acc_ref[...] += chunk(j)
        return carry

    lax.fori_loop(1, n_chunks, body, 0)
    o_ref[...] = x_ref[...] + 0.5 * acc_ref[...]


def _ffn_weights(w_in, w_out, tf=256):
    d, two_ff = w_in.shape
    d_ff = two_ff // 2
    n = d_ff // tf
    w = w_in.astype(_BF16).reshape(d, 2, n, tf).transpose(1, 2, 0, 3)
    return w[0], w[1], w_out.astype(_BF16).reshape(n, tf, d)


def _ffn(x, g, wa, wb, wo):
    t, d = x.shape
    n, _, tf = wa.shape
    tm = _row_tile(t, cap=1024)
    return pl.pallas_call(
        functools.partial(_ffn_kernel, n_chunks=n),
        grid=(t // tm,),
        in_specs=[pl.BlockSpec((tm, d), lambda i: (i, 0)),
                  _resident((1, d)), _resident((n, d, tf)), _resident((n, d, tf)),
                  _resident((n, tf, d))],
        out_specs=pl.BlockSpec((tm, d), lambda i: (i, 0)),
        out_shape=jax.ShapeDtypeStruct((t, d), _F32),
        scratch_shapes=[pltpu.VMEM((tm, d), _F32)],
        compiler_params=_params("arbitrary"),
    )(x, g.reshape(1, d), wa, wb, wo)


def _hgrn_kernel(x_ref, gm_ref, w_ref, wo_ref, s0_ref, lb_ref, on_ref, ltri_ref, utri_ref,
                 y_ref, sfin_ref,
                 st_sc, qs_sc, ks_sc, qd_sc, kd_sc, b_sc, o_sc, v_sc, *, n_sub):
    t = pl.program_id(1)
    w = wo_ref.shape[0]

    @pl.when(t == 0)
    def _init():
        for h in range(HG_HEADS):
            st_sc[h] = s0_ref[0, h].T

    x = x_ref[0]
    xn = (x * lax.rsqrt(jnp.mean(x * x, axis=-1, keepdims=True) + EPS) * gm_ref[...]).astype(_BF16)

    def proj(c):
        return jnp.dot(xn, w_ref[:, c * w:(c + 1) * w], preferred_element_type=_F32)

    lb = lb_ref[...]
    q = proj(0)
    v_sc[...] = proj(2)
    fg = lb + (1.0 - lb) * jax.nn.sigmoid(proj(1))
    logf = jnp.log(jnp.maximum(fg, LOGF_FLOOR))
    kk = 1.0 - fg
    qs = q * jax.nn.sigmoid(q)
    logf_hi = logf.astype(_BF16)
    logf_lo = (logf - logf_hi.astype(_F32)).astype(_BF16)

    def segsum(tri_ref):
        return (jnp.dot(tri_ref[...], logf_hi, preferred_element_type=_F32)
                + jnp.dot(tri_ref[...], logf_lo, preferred_element_type=_F32))

    bcum = segsum(ltri_ref)
    brem = segsum(utri_ref)
    qs_sc[...] = qs
    ks_sc[...] = kk
    b_sc[...] = bcum * LOG2E
    qd_sc[...] = qs * jnp.exp(bcum)
    kd_sc[...] = kk * jnp.exp(brem)

    ones = jnp.ones((HG_DK, LANES), _BF16)
    tio = lax.broadcasted_iota(jnp.int32, (SUB, 1), 0)

    def sub_body(j, carry):
        r0 = pl.multiple_of(j * SUB, SUB)
        rows = pl.ds(r0, SUB)
        for h in range(HG_HEADS):
            cs = slice(h * HG_DK, (h + 1) * HG_DK)
            st = st_sc[h]
            qd = qd_sc[rows, cs]
            acc = lax.dot_general(qd.astype(_BF16), st.astype(_BF16),
                                  (((1,), (1,)), ((), ())), preferred_element_type=_F32)
            bq = b_sc[rows, cs]
            qr = qs_sc[rows, cs]
            kr = ks_sc[rows, cs]
            vv = v_sc[rows, cs]
            terms = []
            for s in range(SUB):
                t0 = 0 if s < HALF else HALF
                dm = jnp.where(tio[t0:] >= s, bq[t0:] - bq[s:s + 1, :], -jnp.inf)
                terms.append(qr[t0:] * kr[s:s + 1, :] * jnp.exp2(dm))
            tcat = jnp.concatenate(terms, axis=0).astype(_BF16)
            rsum = jnp.dot(tcat, ones, preferred_element_type=_F32)
            top, bot, off = acc[:HALF], acc[HALF:], 0
            for s in range(SUB):
                if s < HALF:
                    top = top + rsum[off:off + HALF, :] * vv[s:s + 1, :]
                    off += HALF
                bot = bot + rsum[off:off + HALF, :] * vv[s:s + 1, :]
                off += HALF
            o_sc[rows, cs] = jnp.concatenate([top, bot], axis=0)
            kd = kd_sc[rows, cs]
            upd = jnp.dot(vv.T.astype(_BF16), kd.astype(_BF16), preferred_element_type=_F32)
            st_sc[h] = jnp.exp2(bq[SUB - 1:SUB, :]) * st + upd
        return carry

    lax.fori_loop(0, n_sub, sub_body, 0)

    g = proj(3)
    gate = g * jax.nn.sigmoid(g)
    for h in range(HG_HEADS):
        cs = slice(h * HG_DK, (h + 1) * HG_DK)
        oh = o_sc[:, cs]
        ms = jnp.mean(oh * oh, axis=-1, keepdims=True)
        o_sc[:, cs] = oh * lax.rsqrt(ms + EPS) * on_ref[...] * gate[:, cs]
    y_ref[0] = x_ref[0] + jnp.dot(o_sc[...].astype(_BF16), wo_ref[...], preferred_element_type=_F32)

    @pl.when(t == pl.num_programs(1) - 1)
    def _fin():
        for h in range(HG_HEADS):
            sfin_ref[0, h] = st_sc[h].T


def _hgrn_mixer(x, g_mix, w_in, w_out, s0, lb, onorm, tb):
    bsz, l, d = x.shape
    w = w_in.shape[1] // 4
    dv = w // HG_HEADS
    assert dv == LANES and HG_DK == LANES and tb % SUB == 0 and l % tb == 0
    idx = np.arange(tb)
    same = (idx[:, None] // SUB) == (idx[None, :] // SUB)
    ltri = jnp.asarray((same & (idx[None, :] <= idx[:, None])).astype(np.float32)).astype(_BF16)
    utri = jnp.asarray((same & (idx[None, :] > idx[:, None])).astype(np.float32)).astype(_BF16)

    return pl.pallas_call(
        functools.partial(_hgrn_kernel, n_sub=tb // SUB),
        grid=(bsz, l // tb),
        in_specs=[pl.BlockSpec((1, tb, d), lambda b, t: (b, t, 0)), _resident((1, d)), _resident((d, 4 * w)),
                  _resident((w, d)),
                  pl.BlockSpec((1, HG_HEADS, HG_DK, dv), lambda b, t: (b, 0, 0, 0)),
                  _resident((1, w)), _resident((1, dv)), _resident((tb, tb)), _resident((tb, tb))],
        out_specs=[pl.BlockSpec((1, tb, d), lambda b, t: (b, t, 0)),
                   pl.BlockSpec((1, HG_HEADS, HG_DK, dv), lambda b, t: (b, 0, 0, 0))],
        out_shape=[jax.ShapeDtypeStruct((bsz, l, d), _F32),
                   jax.ShapeDtypeStruct((bsz, HG_HEADS, HG_DK, dv), _F32)],
        scratch_shapes=[pltpu.VMEM((HG_HEADS, dv, HG_DK), _F32)] + [pltpu.VMEM((tb, w), _F32)] * 7,
        compiler_params=_params("arbitrary", "arbitrary"),
    )(x, g_mix.reshape(1, d), w_in, w_out, s0, lb.reshape(1, w), onorm.reshape(1, dv), ltri, utri)


_QW = ATT_HEADS * ATT_DH
_KVW = ATT_KV_HEADS * ATT_DH
_QIW = IDX_HEADS * IDX_DIM
_KI_OFF = _QW + 2 * _KVW + _QIW
_WI_OFF = _KI_OFF + LANES
_DSA_PAD = _WI_OFF + LANES


def _dsa_proj_kernel(x_ref, g_ref, w_ref, qn_ref, kn_ref, c1_ref, s1_ref, c2_ref, s2_ref,
                     q_ref, k_ref, v_ref, qi_ref, ki_ref, wi_ref):
    x = x_ref[...]
    xn = (x * lax.rsqrt(jnp.mean(x * x, axis=-1, keepdims=True) + EPS) * g_ref[...]).astype(_BF16)

    def proj(c0, c1_):
        return jnp.dot(xn, w_ref[:, c0:c1_], preferred_element_type=_F32)

    c1, s1, c2, s2 = c1_ref[...], s1_ref[...], c2_ref[...], s2_ref[...]
    first_half = (lax.broadcasted_iota(jnp.int32, (1, LANES), 1) % IDX_DIM) < (IDX_DIM // 2)

    def norm_rope(x, gain):
        ms = jnp.mean(x * x, axis=-1, keepdims=True)
        x = x * lax.rsqrt(ms + EPS) * gain
        return x * c1 + pltpu.roll(x, ATT_DH // 2, 1) * s1

    def rope_idx(x):
        swapped = jnp.where(first_half, pltpu.roll(x, LANES - IDX_DIM // 2, 1),
                            pltpu.roll(x, IDX_DIM // 2, 1))
        return x * c2 + swapped * s2

    hq = proj(0, _QW)
    for h in range(ATT_HEADS):
        cs = slice(h * ATT_DH, (h + 1) * ATT_DH)
        q_ref[:, cs] = (norm_rope(hq[:, cs], qn_ref[...]) * (ATT_DH ** -0.5 * LOG2E)).astype(_BF16)
    hkv = proj(_QW, _QW + 2 * _KVW)
    for n in range(ATT_KV_HEADS):
        cs = slice(n * ATT_DH, (n + 1) * ATT_DH)
        k_ref[:, cs] = norm_rope(hkv[:, cs], kn_ref[...])
    v_ref[...] = hkv[:, _KVW:]
    hqi = proj(_QW + 2 * _KVW, _KI_OFF)
    for p in range(_QIW // LANES):
        cs = slice(p * LANES, (p + 1) * LANES)
        qi_ref[:, cs] = rope_idx(hqi[:, cs]).astype(_BF16)
    hidx = proj(_KI_OFF, _DSA_PAD)
    ki_ref[...] = rope_idx(hidx[:, :LANES])[:, :IDX_DIM]
    wi_ref[...] = hidx[:, LANES:LANES + IDX_HEADS] * ((IDX_HEADS ** -0.5) * (IDX_DIM ** -0.5))


def _rope_tables(pos):
    def table(d, reps):
        half = d // 2
        inv_freq = jnp.power(ROPE_THETA, -jnp.arange(half, dtype=_F32) * (2.0 / d))
        ang = pos.astype(_F32)[:, None] * inv_freq[None, :]
        c, s = jnp.cos(ang), jnp.sin(ang)
        return (jnp.tile(jnp.concatenate([c, c], axis=-1), (1, reps)),
                jnp.tile(jnp.concatenate([-s, s], axis=-1), (1, reps)))
    c1, s1 = table(ATT_DH, LANES // ATT_DH)
    c2, s2 = table(IDX_DIM, LANES // IDX_DIM)
    return c1, s1, c2, s2


def _dsa_proj(x, g, w, qn, kn, tables, bsz):
    t, d = x.shape
    tm = _row_tile(t)
    n_tab = tables[0].shape[0] // tm
    assert tables[0].shape[0] % tm == 0

    def rows(wd):
        return pl.BlockSpec((tm, wd), lambda i: (i, 0))

    tab = pl.BlockSpec((tm, LANES), lambda i: (i % n_tab, 0))
    outs = pl.pallas_call(
        _dsa_proj_kernel,
        grid=(t // tm,),
        in_specs=[rows(d), _resident((1, d)), _resident((d, _DSA_PAD)),
                  _resident((1, ATT_DH)), _resident((1, ATT_DH)), tab, tab, tab, tab],
        out_specs=[rows(_QW), rows(_KVW), rows(_KVW), rows(_QIW), rows(IDX_DIM), rows(IDX_HEADS)],
        out_shape=[jax.ShapeDtypeStruct((t, _QW), _BF16),
                   jax.ShapeDtypeStruct((t, _KVW), _F32),
                   jax.ShapeDtypeStruct((t, _KVW), _F32),
                   jax.ShapeDtypeStruct((t, _QIW), _BF16),
                   jax.ShapeDtypeStruct((t, IDX_DIM), _F32),
                   jax.ShapeDtypeStruct((t, IDX_HEADS), _F32)],
        compiler_params=_params("arbitrary"),
    )(x, g.reshape(1, d), w, qn.reshape(1, ATT_DH), kn.reshape(1, ATT_DH), *tables)
    return [o.reshape(bsz, t // bsz, o.shape[-1]) for o in outs]


def _dsa_in_weight(w_in):
    d = w_in.shape[0]
    z = lambda n: jnp.zeros((d, n), w_in.dtype)
    ki0 = _QW + 2 * _KVW + _QIW
    return jnp.concatenate([w_in[:, :ki0], w_in[:, ki0:ki0 + IDX_DIM], z(LANES - IDX_DIM),
                            w_in[:, ki0 + IDX_DIM:], z(LANES - IDX_HEADS)], axis=1).astype(_BF16)


def _dsa_attn_kernel(bound_ref, q_ref, qi_ref, wi_ref, k_ref, vt_ref, ki_ref, res_ref, wo_ref, o_ref,
                     key_sc, cmax_sc, acc_sc, thr_sc, *, qb_size, topk, causal, lk_true, n_valid_q):
    i = pl.program_id(1)
    n_blk = pl.num_programs(1) - 1
    nq = qb_size
    has_new = i < n_blk
    has_old = i >= 1
    cur = i % 2
    key_cur = key_sc.at[cur]
    key_old = key_sc.at[1 - cur]
    lane_q = lax.broadcasted_iota(jnp.int32, (1, nq), 1)

    def key_steps(blk):
        if causal:
            return ((blk + 1) * nq + KS - 1) // KS
        return k_ref.shape[1] // KS

    n_s = jnp.where(has_new, key_steps(i), 0)
    n_a = jnp.where(has_old, key_steps(i - 1), 0)
    if causal:
        limit = ((i * nq + lane_q) // CHUNK + 1) * CHUNK
    else:
        limit = jnp.full((1, nq), lk_true, jnp.int32)
    row_io = lax.broadcasted_iota(jnp.int32, (LANES, 1), 0)

    @pl.when(has_new)
    def _scores():
        qi_t = qi_ref[0].astype(_F32).T.astype(_BF16)
        qi_pairs = [jnp.concatenate([qi_t[(2 * p + e) * IDX_DIM:(2 * p + e + 1) * IDX_DIM, :]
                                     for e in range(2)], axis=1) for p in range(IDX_HEADS // 2)]
        wi = wi_ref[0]
        cmax_sc[...] = jnp.full(cmax_sc.shape, INT_MIN, jnp.int32)

        def score_body(t, carry):
            for u in range(KS // LANES):
                r0 = pl.multiple_of(t * KS + u * LANES, LANES)
                kit = ki_ref[0, pl.ds(r0, LANES), :]
                acc = jnp.zeros((LANES, nq), _F32)
                for p in range(IDX_HEADS // 2):
                    d = jnp.dot(kit, qi_pairs[p], preferred_element_type=_F32)
                    for e in range(2):
                        h = 2 * p + e
                        acc = acc + jnp.maximum(d[:, e * nq:(e + 1) * nq], 0.0) * wi[h:h + 1, :]
                acc = jnp.where(r0 + row_io < limit, acc, -jnp.inf)
                bits = pltpu.bitcast(acc, jnp.int32)
                key = bits ^ ((bits >> 31) & 0x7FFFFFFF)
                key_cur[pl.ds(r0, LANES), :] = key
                cls = slice((u % (NCLS // LANES)) * LANES, (u % (NCLS // LANES) + 1) * LANES)
                cmax_sc[cls, :] = jnp.maximum(cmax_sc[cls, :], key)
            return carry

        lax.fori_loop(0, n_s, score_body, 0)

    n_cs = n_s * (KS // CS)

    def count_chunk(c, cand, piece):
        r0 = pl.multiple_of(piece * CS, CS)
        m = (key_cur[pl.ds(r0, CS), :] >= cand).astype(jnp.int32)
        return c + jnp.sum(m.reshape(CS // 8, 8, nq), axis=0)

    def count_from(cand, first, c):
        c = lax.fori_loop(first, n_cs, lambda p, c: count_chunk(c, cand, p), c)
        return jnp.sum(c, axis=0, keepdims=True)

    def active(lo, hi, cnt_lo):
        return jnp.where((cnt_lo != topk) & (hi - 1 > lo), 1.0, 0.0)

    def midpoint(lo, hi):
        return (lo >> 1) + (hi >> 1) + (lo & hi & 1)

    def narrow(st, cand, c, gate):
        lo, hi, cnt_lo, cnt_hi = st
        act = (active(lo, hi, cnt_lo) > 0) & gate
        up = act & (c >= topk)
        dn = act & (c < topk)
        return (jnp.where(up, cand, lo), jnp.where(dn, cand, hi),
                jnp.where(up, c, cnt_lo), jnp.where(dn, c, cnt_hi))

    zero8 = jnp.zeros((8, nq), jnp.int32)
    always = lane_q >= 0
    cm = cmax_sc[...]
    lo0 = jnp.min(cm, axis=0, keepdims=True)
    top = jnp.max(cm, axis=0, keepdims=True)
    hi0 = jnp.where(top == 2 ** 31 - 1, top, top + 1)
    state = (lo0, hi0, jnp.full((1, nq), n_s * KS + topk + 1, jnp.int32), jnp.zeros((1, nq), jnp.int32))

    thr_old = thr_sc[1 - cur][0:1, :]
    q_t = q_ref[0].astype(_F32).T.astype(_BF16)
    n_pair = ATT_GROUP // HPS
    rhs = [[jnp.concatenate([q_t[(n * ATT_GROUP + HPS * pr + e) * ATT_DH:(n * ATT_GROUP + HPS * pr + e + 1) * ATT_DH, :]
                             for e in range(HPS)], axis=1) for pr in range(n_pair)]
           for n in range(ATT_KV_HEADS)]
    n_slot = ATT_KV_HEADS * n_pair

    def step_bias(t):
        r0 = pl.multiple_of(t * KS, KS)
        bias = jnp.where(key_old[pl.ds(r0, KS), :] >= thr_old, 0.0, NEG_BIG).astype(_BF16)
        return r0, jnp.concatenate([bias] * HPS, axis=1)

    def attend_unshifted(t, between=None):
        r0, bias2 = step_bias(t)
        for n in range(ATT_KV_HEADS):
            if between is not None:
                between(n)
            k_rows = k_ref[0, pl.ds(r0, KS), n * ATT_DH:(n + 1) * ATT_DH]
            for pr in range(n_pair):
                slot = n * n_pair + pr
                s = jnp.dot(k_rows, rhs[n][pr], preferred_element_type=_F32)
                p = jnp.exp2(s.astype(_BF16) + bias2)
                if between is not None:
                    between(n)
                acc_sc[slot] += jnp.dot(vt_ref[0, t, n], p, preferred_element_type=_F32)

    fast = bound_ref[0] <= LOGIT_SAFE
    acc_sc[...] = jnp.zeros_like(acc_sc)

    last_piece = jnp.maximum(n_cs - 1, 0)

    def fused_body(t, carry):
        box = [carry[:4], carry[4], carry[5], carry[6]]

        def counts(_n):
            st, cand, c8, piece = box
            for _ in range(FUSED_COUNTS // (2 * ATT_KV_HEADS)):
                c8 = count_chunk(c8, cand, jnp.minimum(piece, last_piece))
                piece = piece + 1
                done = piece >= n_cs
                gate = (jnp.zeros((1, nq), jnp.int32) + done.astype(jnp.int32)) > 0
                st = narrow(st, cand, jnp.sum(c8, axis=0, keepdims=True), gate)
                cand = jnp.where(gate, midpoint(st[0], st[1]), cand)
                c8 = jnp.where(jnp.zeros((8, nq), jnp.int32) + done.astype(jnp.int32) > 0, 0, c8)
                piece = jnp.where(done, 0, piece)
            box[:] = [st, cand, c8, piece]

        attend_unshifted(t, counts)
        st, cand, c8, piece = box
        return st + (cand, c8, piece)

    carry = lax.fori_loop(0, jnp.where(fast, n_a, 0), fused_body,
                          state + (midpoint(lo0, hi0), zero8, jnp.int32(0)))
    state, cand, c8, piece = carry[:4], carry[4], carry[5], carry[6]

    @pl.when(jnp.logical_not(fast) & has_old)
    def _attend_two_pass():
        def max_body(t, ms):
            r0, bias2 = step_bias(t)
            out = []
            for n in range(ATT_KV_HEADS):
                k_rows = k_ref[0, pl.ds(r0, KS), n * ATT_DH:(n + 1) * ATT_DH]
                for pr in range(n_pair):
                    s = jnp.dot(k_rows, rhs[n][pr], preferred_element_type=_F32).astype(_BF16) + bias2
                    out.append(jnp.maximum(ms[n * n_pair + pr],
                                           jnp.max(s, axis=0, keepdims=True).astype(_F32)))
            return tuple(out)

        ms = lax.fori_loop(0, n_a, max_body,
                           tuple(jnp.full((1, HPS * nq), NEG_BIG, _F32) for _ in range(n_slot)))

        first_row = lax.broadcasted_iota(jnp.int32, (ATT_DH, 1), 0) == 0
        ones_col = jnp.where(lax.broadcasted_iota(jnp.int32, (KS, ATT_DH), 1) == 0, 1.0, 0.0).astype(_BF16)
        rhs_aug = []
        for n in range(ATT_KV_HEADS):
            for pr in range(n_pair):
                m_ref = jnp.where(ms[n * n_pair + pr] > 0.5 * NEG_BIG, ms[n * n_pair + pr], 0.0)
                shift = jnp.where(first_row, -m_ref, 0.0).astype(_BF16)
                rhs_aug.append(jnp.concatenate([rhs[n][pr], shift], axis=0))

        def pv_body(t, carry):
            r0, bias2 = step_bias(t)
            for n in range(ATT_KV_HEADS):
                k_aug = jnp.concatenate([k_ref[0, pl.ds(r0, KS), n * ATT_DH:(n + 1) * ATT_DH], ones_col], axis=1)
                for pr in range(n_pair):
                    slot = n * n_pair + pr
                    s = jnp.dot(k_aug, rhs_aug[slot], preferred_element_type=_F32)
                    p = jnp.exp2(s.astype(_BF16) + bias2)
                    acc_sc[slot] += jnp.dot(vt_ref[0, t, n], p, preferred_element_type=_F32)
            return carry

        lax.fori_loop(0, n_a, pv_body, 0)

    @pl.when(has_old)
    def _write_out():
        heads = []
        for n in range(ATT_KV_HEADS):
            for pr in range(n_pair):
                a = acc_sc[n * n_pair + pr]
                o_t = a[:ATT_DH, :] / a[ATT_DH:ATT_DH + 1, :]
                heads += [o_t[:, e * nq:(e + 1) * nq].T.astype(_BF16) for e in range(HPS)]
        o = jnp.concatenate(heads, axis=1)
        o_ref[0] = res_ref[0] + jnp.dot(o, wo_ref[...], preferred_element_type=_F32)

    state = narrow(state, cand, count_from(cand, piece, c8), always)

    def halve(st):
        cand = midpoint(st[0], st[1])
        return narrow(st, cand, count_from(cand, 0, zero8), always)

    def search_body(st):
        st = st[:4]
        for _ in range(SEARCH_UNROLL):
            st = halve(st)
        return st + (jnp.max(active(st[0], st[1], st[2])),)

    thr, _, n_ge, n_gt, _ = lax.while_loop(
        lambda st: st[4] > 0, search_body,
        state + (jnp.max(active(state[0], state[1], state[2])),))

    need = (topk - n_gt).astype(_F32)
    over = jnp.max(jnp.where(lane_q < n_valid_q, n_ge, 0)) > topk

    @pl.when(over & has_new)
    def _drop_ties():
        rr = lax.broadcasted_iota(jnp.int32, (KS, KS), 0)
        cc = lax.broadcasted_iota(jnp.int32, (KS, KS), 1)
        strict = jnp.where(cc < rr, 1.0, 0.0).astype(_BF16)

        def body(t, run):
            r0 = pl.multiple_of(t * KS, KS)
            kk = key_cur[pl.ds(r0, KS), :]
            tie = kk == thr
            tf = jnp.where(tie, 1.0, 0.0)
            before = jnp.dot(strict, tf.astype(_BF16), preferred_element_type=_F32) + run
            key_cur[pl.ds(r0, KS), :] = jnp.where(tie & (before >= need), KEY_NEG_INF, kk)
            return run + jnp.sum(tf, axis=0, keepdims=True)

        lax.fori_loop(0, n_s, body, jnp.zeros((1, nq), _F32))

    thr_sc[cur] = jnp.broadcast_to(jnp.maximum(thr, KEY_NEG_INF + 1), (8, nq))


def _dsa_attn(q, qi, wi, k, v, ki, res, w_out, logit_bound, *, causal, lk_true, topk, n_valid_q, qb_size=128):
    bsz, lq, _ = q.shape
    d_out = w_out.shape[1]
    lk = k.shape[1]
    ns = lk // KS
    n_blk = lq // qb_size
    assert lq % qb_size == 0 and lk % KS == 0 and KS % CS == 0 and KS % NCLS == 0 and topk <= NCLS
    wi_t = jnp.swapaxes(wi, 1, 2)
    vt = v.astype(_BF16).reshape(bsz, ns, KS, ATT_KV_HEADS, ATT_DH).transpose(0, 1, 3, 4, 2)
    ones_rows = jnp.zeros((bsz, ns, ATT_KV_HEADS, 8, KS), _BF16).at[:, :, :, 0, :].set(1.0)
    vt = jnp.concatenate([vt, ones_rows], axis=3)

    def whole(shape):
        nd = len(shape)
        return pl.BlockSpec((1,) + shape, lambda b, i: (b,) + (0,) * nd, pipeline_mode=pl.Buffered(1))

    def old_rows(wd):
        return pl.BlockSpec((1, qb_size, wd), lambda b, i: (b, jnp.maximum(i - 1, 0), 0))

    def new_rows(wd):
        return pl.BlockSpec((1, qb_size, wd), lambda b, i: (b, jnp.minimum(i, n_blk - 1), 0))

    return pl.pallas_call(
        functools.partial(_dsa_attn_kernel, qb_size=qb_size, topk=topk, causal=causal,
                          lk_true=lk_true, n_valid_q=n_valid_q),
        grid=(bsz, n_blk + 1),
        in_specs=[pl.BlockSpec(memory_space=pltpu.SMEM), old_rows(_QW), new_rows(_QIW),
                  pl.BlockSpec((1, IDX_HEADS, qb_size), lambda b, i: (b, 0, jnp.minimum(i, n_blk - 1))),
                  whole((lk, _KVW)), whole((ns, ATT_KV_HEADS, ATT_DH + 8, KS)), whole((lk, IDX_DIM)),
                  old_rows(d_out), _resident((_QW, d_out))],
        out_specs=old_rows(d_out),
        out_shape=jax.ShapeDtypeStruct((bsz, lq, d_out), _F32),
        scratch_shapes=[pltpu.VMEM((2, lk, qb_size), jnp.int32),
                        pltpu.VMEM((NCLS, qb_size), jnp.int32),
                        pltpu.VMEM((ATT_KV_HEADS * ATT_GROUP // HPS, ATT_DH + 8, HPS * qb_size), _F32),
                        pltpu.VMEM((2, 8, qb_size), jnp.int32)],
        compiler_params=_params("arbitrary", "arbitrary"),
    )(jnp.reshape(logit_bound, (1,)).astype(_F32), q, qi, wi_t, k.astype(_BF16), vt, ki.astype(_BF16),
      res, w_out)


def _hgrn_lower_bounds(logits):
    p = jax.nn.softmax(logits.astype(_F32), axis=0)
    return jnp.maximum(jnp.cumsum(p, axis=0) - p[:1], 0.0)


def _pad_rows(x, n):
    return jnp.pad(x, ((0, 0), (0, n - x.shape[1]), (0, 0)))


def kernel(x_prompt, x_sample, state_hgrn, cache_k, cache_v, cache_kidx, norm_ffn1, ffn1_w_in, ffn1_w_out, norm_mix, norm_ffn2, ffn2_w_in, ffn2_w_out, hgrn_w_in, hgrn_lb_logits, hgrn_onorm, hgrn_w_out, dsa_w_in, dsa_qnorm, dsa_knorm, dsa_w_out):
    bp, lp, d = x_prompt.shape
    bs, ls, _ = x_sample.shape
    past = cache_k.shape[2]
    depth = norm_ffn1.shape[0]
    assert ls == SUB and lp % 256 == 0
    lb_all = _hgrn_lower_bounds(hgrn_lb_logits)
    tab_p = _rope_tables(jnp.arange(lp, dtype=jnp.int32))
    tab_s = tuple(jnp.tile(a, (bs, 1)) for a in _rope_tables(past + jnp.arange(ls, dtype=jnp.int32)))
    lk_s = past + ls
    qb = 128
    lk_pad = -(-lk_s // KS) * KS

    xp = x_prompt.reshape(bp * lp, d)
    xs = x_sample.reshape(bs * ls, d)
    hg_p, hg_s, kp, vp, kip, ksm, vsm, kism = [], [], [], [], [], [], [], []
    for i in range(depth):
        w1 = _ffn_weights(ffn1_w_in[i], ffn1_w_out[i])
        xp = _ffn(xp, norm_ffn1[i], *w1)
        xs = _ffn(xs, norm_ffn1[i], *w1)
        j = i // 2
        if i % 2 == 0:
            w_in = hgrn_w_in[j].astype(_BF16)
            w_out = hgrn_w_out[j].astype(_BF16)
            s0 = jnp.zeros((bp,) + state_hgrn.shape[2:], _F32)
            xp, sp = _hgrn_mixer(xp.reshape(bp, lp, d), norm_mix[i], w_in, w_out, s0, lb_all[j],
                                 hgrn_onorm[j], tb=256)
            xs, ss = _hgrn_mixer(xs.reshape(bs, ls, d), norm_mix[i], w_in, w_out, state_hgrn[j], lb_all[j],
                                 hgrn_onorm[j], tb=ls)
            xp = xp.reshape(bp * lp, d)
            xs = xs.reshape(bs * ls, d)
            hg_p.append(sp)
            hg_s.append(ss)
        else:
            w_in = _dsa_in_weight(dsa_w_in[j])
            w_out = dsa_w_out[j].astype(_BF16)
            q1, k1, v1, qi1, ki1, wi1 = _dsa_proj(xp, norm_mix[i], w_in, dsa_qnorm[j], dsa_knorm[j], tab_p, bp)
            q2, k2, v2, qi2, ki2, wi2 = _dsa_proj(xs, norm_mix[i], w_in, dsa_qnorm[j], dsa_knorm[j], tab_s, bs)
            bound = (1.01 * ATT_DH ** 0.5 * LOG2E) * jnp.max(jnp.abs(dsa_qnorm[j])) * jnp.max(jnp.abs(dsa_knorm[j]))
            xp_new = _dsa_attn(q1, qi1, wi1, k1, v1, ki1, xp.reshape(bp, lp, d), w_out, bound, causal=True,
                               lk_true=lp, topk=min(TOPK_MAX, lp // 4), n_valid_q=qb, qb_size=qb)
            k_all = _pad_rows(jnp.concatenate([cache_k[j].reshape(bs, past, _KVW), k2], axis=1), lk_pad)
            v_all = _pad_rows(jnp.concatenate([cache_v[j].reshape(bs, past, _KVW), v2], axis=1), lk_pad)
            ki_all = _pad_rows(jnp.concatenate([cache_kidx[j], ki2], axis=1), lk_pad)
            xs_new = _dsa_attn(_pad_rows(q2, qb), _pad_rows(qi2, qb), _pad_rows(wi2, qb), k_all, v_all, ki_all,
                               _pad_rows(xs.reshape(bs, ls, d), qb), w_out, jnp.inf, causal=False, lk_true=lk_s,
                               topk=min(TOPK_MAX, lk_s // 4), n_valid_q=ls, qb_size=qb)[:, :ls]
            kp.append(k1.reshape(bp, lp, ATT_KV_HEADS, ATT_DH))
            vp.append(v1.reshape(bp, lp, ATT_KV_HEADS, ATT_DH))
            kip.append(ki1)
            ksm.append(k2.reshape(bs, ls, ATT_KV_HEADS, ATT_DH))
            vsm.append(v2.reshape(bs, ls, ATT_KV_HEADS, ATT_DH))
            kism.append(ki2)
            xp = xp_new.reshape(bp * lp, d)
            xs = xs_new.reshape(bs * ls, d)
        w2 = _ffn_weights(ffn2_w_in[i], ffn2_w_out[i])
        xp = _ffn(xp, norm_ffn2[i], *w2)
        xs = _ffn(xs, norm_ffn2[i], *w2)
    return (xp.reshape(bp, lp, d), xs.reshape(bs, ls, d), jnp.stack(hg_p),
            jnp.stack(kp), jnp.stack(vp), jnp.stack(kip), jnp.stack(hg_s),
            jnp.stack(ksm), jnp.stack(vsm), jnp.stack(kism))
```
